```python
import math
import jax, jax.numpy as jnp
from jax import lax
import numpy as np

D_MODEL = 1024
BATCH = 2
SEQ = 16384
DEPTH = 2

RG_WIDTH = 512
RG_BLOCKS = 8
RG_BLOCK_DIM = RG_WIDTH // RG_BLOCKS
CONV_WIDTH = 4
RG_C = 8.0
GLA_HEADS = 4
GLA_DK = 64
GLA_DV = 128
GLA_RANK = 16
GLA_GATE_NORMALIZER = 16.0
HG_HEADS = 4
HG_DK = 128
HG_DV = 128
HG_F_MIN = 1e-30
N_BRANCHES = 3
BRANCH_WIDTH = 512
CHUNK = 64
D_FF = 2816
N_EXPERTS = 8
TOP_K = 2
D_FF_EXPERT = 3584
MOE_BLOCK = 512
N_DENSE = (DEPTH + 1) // 2
N_MOE = DEPTH // 2
ALPHA = (2 * DEPTH) ** 0.25
BETA = (8 * DEPTH) ** -0.25
LN_EPS = 1e-5
RMS_EPS = 1e-6

SPLIT_SIZES = (RG_WIDTH,
               GLA_HEADS * GLA_DK, GLA_HEADS * GLA_DK, GLA_HEADS * GLA_DV, GLA_RANK, GLA_HEADS * GLA_DV,
               HG_HEADS * HG_DK, HG_HEADS * HG_DK, HG_HEADS * HG_DV, HG_HEADS * HG_DV,
               N_BRANCHES * D_MODEL)
D_IN = sum(SPLIT_SIZES)
SPLIT_POINTS = tuple(sum(SPLIT_SIZES[:i + 1]) for i in range(len(SPLIT_SIZES) - 1))

kernel_name = 'hybrid_rglru_gla_hgrn2_moe_deepnorm'


def layer_norm(x, g, b):
    xf = x.astype(jnp.float32)
    mu = jnp.mean(xf, axis=-1, keepdims=True)
    var = jnp.mean(jnp.square(xf - mu), axis=-1, keepdims=True)
    y = (xf - mu) * lax.rsqrt(var + LN_EPS) * g.astype(jnp.float32) + b.astype(jnp.float32)
    return y.astype(x.dtype)


def rms_norm(x, w):
    xf = x.astype(jnp.float32)
    return xf * lax.rsqrt(jnp.mean(xf * xf, axis=-1, keepdims=True) + RMS_EPS) * w.astype(jnp.float32)


def chunked_gated_linear_attention(q, k, v, log_g, scale):
    B, S, H, K = q.shape
    V = v.shape[-1]
    N = S // CHUNK

    def chunks(t):
        return t.astype(jnp.float32).reshape(B, N, CHUNK, H, t.shape[-1]).transpose(1, 0, 3, 2, 4)

    qc, kc, vc = chunks(q * scale), chunks(k), chunks(v)
    G = jnp.cumsum(chunks(log_g), axis=3)
    causal = jnp.tril(jnp.ones((CHUNK, CHUNK), dtype=bool))[:, :, None]

    def step(state, inp):
        qi, ki, vi, Gi = inp
        o_inter = jnp.einsum('bhck,bhkv->bhcv', qi * jnp.exp(Gi), state)
        diff = jnp.where(causal, Gi[:, :, :, None, :] - Gi[:, :, None, :, :], 0.0)
        decay = jnp.where(causal, jnp.exp(diff), 0.0)
        scores = jnp.einsum('bhck,bhsk,bhcsk->bhcs', qi, ki, decay)
        o_intra = jnp.einsum('bhcs,bhsv->bhcv', scores, vi)
        G_last = Gi[:, :, -1:, :]
        k_dec = ki * jnp.exp(G_last - Gi)
        state = state * jnp.exp(G_last[:, :, 0, :, None]) + jnp.einsum('bhsk,bhsv->bhkv', k_dec, vi)
        return state, o_inter + o_intra

    state0 = jnp.zeros((B, H, K, V), jnp.float32)
    _, o = lax.scan(step, state0, (qc, kc, vc, G))
    return o.transpose(1, 0, 3, 2, 4).reshape(B, S, H, V)


def rglru_branch(u, conv_w, conv_b, w_r, b_r, w_i, b_i, lam):
    B, S, W = u.shape
    u_pad = jnp.pad(u, ((0, 0), (CONV_WIDTH - 1, 0), (0, 0)))
    c = conv_b
    for j in range(CONV_WIDTH):
        c = c + u_pad[:, j:j + S] * conv_w[j]
    cb = c.reshape(B, S, RG_BLOCKS, RG_BLOCK_DIM)
    r = jax.nn.sigmoid(jnp.einsum('bsgi,gij->bsgj', cb, w_r).reshape(B, S, W) + b_r).astype(jnp.float32)
    i = jax.nn.sigmoid(jnp.einsum('bsgi,gij->bsgj', cb, w_i).reshape(B, S, W) + b_i).astype(jnp.float32)
    log_a = -RG_C * r * jax.nn.softplus(-lam.astype(jnp.float32))
    a = jnp.exp(log_a)
    b = jnp.sqrt(jnp.maximum(-jnp.expm1(2.0 * log_a), 0.0)) * (i * c.astype(jnp.float32))

    def combine(left, right):
        return left[0] * right[0], right[0] * left[1] + right[1]

    _, h = lax.associative_scan(combine, (a, b), axis=1)
    return h.astype(u.dtype)


def gla_branch(q, k, v, a_low, g, w_a, b_a, norm_w):
    B, S, _ = q.shape
    log_alpha = jax.nn.log_sigmoid((a_low @ w_a + b_a).astype(jnp.float32)) / GLA_GATE_NORMALIZER
    o = chunked_gated_linear_attention(
        q.reshape(B, S, GLA_HEADS, GLA_DK), k.reshape(B, S, GLA_HEADS, GLA_DK),
        v.reshape(B, S, GLA_HEADS, GLA_DV), log_alpha.reshape(B, S, GLA_HEADS, GLA_DK), GLA_DK ** -0.5)
    o = rms_norm(o, norm_w).reshape(B, S, GLA_HEADS * GLA_DV)
    return (o * jax.nn.silu(g.astype(jnp.float32))).astype(q.dtype)


def hgrn2_branch(q, f, i, g, lower_bound, norm_w):
    B, S, _ = q.shape
    z = f.astype(jnp.float32)
    forget = lower_bound + (1.0 - lower_bound) * jax.nn.sigmoid(z)
    log_f = jnp.log(jnp.maximum(forget, HG_F_MIN))
    k = (1.0 - lower_bound) * jax.nn.sigmoid(-z)
    qs = jax.nn.silu(q.astype(jnp.float32))
    o = chunked_gated_linear_attention(
        qs.reshape(B, S, HG_HEADS, HG_DK), k.reshape(B, S, HG_HEADS, HG_DK),
        i.reshape(B, S, HG_HEADS, HG_DV), log_f.reshape(B, S, HG_HEADS, HG_DK), HG_DK ** -0.5)
    o = rms_norm(o.reshape(B, S, HG_HEADS * HG_DV), norm_w)
    return (o * jax.nn.sigmoid(g.astype(jnp.float32))).astype(q.dtype)


def token_mixer(x, w_in, conv_w, conv_b, rg_wr, rg_br, rg_wi, rg_bi, rg_lambda,
                gla_wa, gla_ba, gla_norm_w, lower_bound, hg_norm_w, w_branch, w_out):
    B, S, _ = x.shape
    proj = x @ w_in
    (rg_x, gq, gk, gv, ga, gg, hq, hf, hi, hg, gates) = jnp.split(proj, SPLIT_POINTS, axis=-1)
    y_rg = rglru_branch(rg_x, conv_w, conv_b, rg_wr, rg_br, rg_wi, rg_bi, rg_lambda)
    y_gla = gla_branch(gq, gk, gv, ga, gg, gla_wa, gla_ba, gla_norm_w)
    y_hg = hgrn2_branch(hq, hf, hi, hg, lower_bound, hg_norm_w)
    ys = jnp.stack([y_rg, y_gla, y_hg], axis=2)
    branch_out = jnp.einsum('bsnw,nwd->bsnd', ys, w_branch)
    merge = jax.nn.sigmoid(gates.reshape(B, S, N_BRANCHES, D_MODEL))
    mixed = jnp.sum(merge * branch_out, axis=2)
    return mixed @ w_out


def swiglu(x, w_gate, w_up, w_down):
    return (jax.nn.silu(x @ w_gate) * (x @ w_up)) @ w_down


def moe_swiglu(x, router_w, w_gate, w_up, w_down):
    B, S, D = x.shape
    T = B * S
    TK = T * TOP_K
    xf = x.reshape(T, D)
    logits = (xf @ router_w).astype(jnp.float32)
    top_logits, top_idx = lax.top_k(logits, TOP_K)
    top_w = jax.nn.softmax(top_logits, axis=-1)
    expert = top_idx.reshape(-1).astype(jnp.int32)
    token = jnp.repeat(jnp.arange(T, dtype=jnp.int32), TOP_K)
    weight = top_w.reshape(-1)
    order = jnp.argsort(expert, stable=True)
    e_sorted = expert[order]
    counts = jnp.bincount(expert, length=N_EXPERTS).astype(jnp.int32)
    padded = (counts + MOE_BLOCK - 1) // MOE_BLOCK * MOE_BLOCK
    start = jnp.cumsum(counts) - counts
    pad_end = jnp.cumsum(padded)
    pad_start = pad_end - padded
    dest = pad_start[e_sorted] + jnp.arange(TK, dtype=jnp.int32) - start[e_sorted]
    n_blocks = -(-TK // MOE_BLOCK) + N_EXPERTS
    P = n_blocks * MOE_BLOCK
    buf_tok = jnp.zeros((P,), jnp.int32).at[dest].set(token[order])
    buf_w = jnp.zeros((P,), jnp.float32).at[dest].set(weight[order])
    block_start = jnp.arange(n_blocks, dtype=jnp.int32) * MOE_BLOCK
    block_expert = jnp.minimum(jnp.searchsorted(pad_end, block_start, side='right'), N_EXPERTS - 1)
    xs = xf[buf_tok].reshape(n_blocks, MOE_BLOCK, D)

    def expert_block(args):
        xb, e = args
        return swiglu(xb, w_gate[e], w_up[e], w_down[e])

    ys = lax.map(expert_block, (xs, block_expert)).reshape(P, D)
    y = jnp.zeros((T, D), ys.dtype).at[buf_tok].add(ys * buf_w[:, None].astype(ys.dtype))
    return y.reshape(B, S, D).astype(x.dtype)


def setup_inputs(seed: int = 0) -> dict:
    key = jax.random.key(seed)
    ks = jax.random.split(key, 28)
    f32 = jnp.float32

    def dense(k, shape, fan_in, scale=1.0):
        return jax.random.normal(k, shape, f32) * (scale * fan_in ** -0.5)

    def small(k, shape):
        return 0.01 * jax.random.normal(k, shape, f32)

    def gain(k, shape):
        return 1.0 + 0.01 * jax.random.normal(k, shape, f32)

    hk = GLA_HEADS * GLA_DK
    a_pow = jax.random.uniform(ks[9], (DEPTH, RG_WIDTH), f32, 0.9, 0.999)
    a_base = a_pow ** (1.0 / RG_C)
    return {
        'x': jax.random.normal(ks[0], (BATCH, SEQ, D_MODEL), f32),
        'w_in': dense(ks[1], (DEPTH, D_MODEL, D_IN), D_MODEL),
        'conv_w': dense(ks[2], (DEPTH, CONV_WIDTH, RG_WIDTH), CONV_WIDTH),
        'conv_b': small(ks[3], (DEPTH, RG_WIDTH)),
        'rg_wr': dense(ks[4], (DEPTH, RG_BLOCKS, RG_BLOCK_DIM, RG_BLOCK_DIM), RG_BLOCK_DIM),
        'rg_br': small(ks[5], (DEPTH, RG_WIDTH)),
        'rg_wi': dense(ks[6], (DEPTH, RG_BLOCKS, RG_BLOCK_DIM, RG_BLOCK_DIM), RG_BLOCK_DIM),
        'rg_bi': small(ks[7], (DEPTH, RG_WIDTH)),
        'rg_lambda': jnp.log(a_base) - jnp.log1p(-a_base),
        'gla_wa': dense(ks[8], (DEPTH, GLA_RANK, hk), GLA_RANK),
        'gla_ba': small(ks[10], (DEPTH, hk)),
        'gla_norm_w': gain(ks[11], (DEPTH, GLA_DV)),
        'hg_lb_logits': 0.1 * jax.random.normal(ks[12], (DEPTH, HG_HEADS * HG_DK), f32),
        'hg_norm_w': gain(ks[13], (DEPTH, HG_HEADS * HG_DV)),
        'w_branch': dense(ks[14], (DEPTH, N_BRANCHES, BRANCH_WIDTH, D_MODEL), BRANCH_WIDTH, BETA),
        'w_out': dense(ks[15], (DEPTH, D_MODEL, D_MODEL), D_MODEL, BETA),
        'ln1_g': gain(ks[16], (DEPTH, D_MODEL)),
        'ln1_b': small(ks[17], (DEPTH, D_MODEL)),
        'ln2_g': gain(ks[18], (DEPTH, D_MODEL)),
        'ln2_b': small(ks[19], (DEPTH, D_MODEL)),
        'ffn_wg': dense(ks[20], (N_DENSE, D_MODEL, D_FF), D_MODEL),
        'ffn_wu': dense(ks[21], (N_DENSE, D_MODEL, D_FF), D_MODEL),
        'ffn_wd': dense(ks[22], (N_DENSE, D_FF, D_MODEL), D_FF, BETA),
        'router_w': dense(ks[23], (N_MOE, D_MODEL, N_EXPERTS), D_MODEL),
        'moe_wg': dense(ks[24], (N_MOE, N_EXPERTS, D_MODEL, D_FF_EXPERT), D_MODEL),
        'moe_wu': dense(ks[25], (N_MOE, N_EXPERTS, D_MODEL, D_FF_EXPERT), D_MODEL),
        'moe_wd': dense(ks[26], (N_MOE, N_EXPERTS, D_FF_EXPERT, D_MODEL), D_FF_EXPERT, BETA),
    }


def reference(x, w_in, conv_w, conv_b, rg_wr, rg_br, rg_wi, rg_bi, rg_lambda,
              gla_wa, gla_ba, gla_norm_w, hg_lb_logits, hg_norm_w, w_branch, w_out,
              ln1_g, ln1_b, ln2_g, ln2_b, ffn_wg, ffn_wu, ffn_wd,
              router_w, moe_wg, moe_wu, moe_wd):
    lb_w = jax.nn.softmax(hg_lb_logits.astype(jnp.float32), axis=0)
    lower_bounds = jnp.cumsum(lb_w, axis=0) - lb_w[0]
    for l in range(DEPTH):
        h = token_mixer(x, w_in[l], conv_w[l], conv_b[l], rg_wr[l], rg_br[l], rg_wi[l], rg_bi[l],
                        rg_lambda[l], gla_wa[l], gla_ba[l], gla_norm_w[l], lower_bounds[l],
                        hg_norm_w[l], w_branch[l], w_out[l])
        x = layer_norm(ALPHA * x + h, ln1_g[l], ln1_b[l])
        j = l // 2
        if l % 2 == 0:
            f = swiglu(x, ffn_wg[j], ffn_wu[j], ffn_wd[j])
        else:
            f = moe_swiglu(x, router_w[j], moe_wg[j], moe_wu[j], moe_wd[j])
        x = layer_norm(ALPHA * x + f, ln2_g[l], ln2_b[l])
    return x
```

```python
import functools

import jax
import jax.numpy as jnp
from jax import lax
from jax.experimental import pallas as pl
from jax.experimental.pallas import tpu as pltpu

F32 = jnp.float32
MXU_DTYPE = jnp.bfloat16

RG_C = 8.0
GLA_HEADS, GLA_DK, GLA_DV = 4, 64, 128
GLA_GATE_NORMALIZER = 16.0
HG_HEADS, HG_DK, HG_DV = 4, 128, 128
HG_F_MIN = 1e-30
TOP_K = 2
LN_EPS = 1e-5
RMS_EPS = 1e-6

CHUNK = 64
SUB = 16
PAD = SUB
SAFE_EXP = 60.0

V7X_LANES = 128
V7X_MXU_DIM = 256
V7X_VMEM_BYTES = 64 * 1024 * 1024
VMEM_LIMIT = V7X_VMEM_BYTES * 7 // 8


def _cparams(*sem):
    return pltpu.CompilerParams(dimension_semantics=sem, vmem_limit_bytes=VMEM_LIMIT)


def _mm(a, b):
    return jnp.dot(a.astype(MXU_DTYPE), b.astype(MXU_DTYPE), preferred_element_type=F32)


def _mm_nt(a, b):
    return lax.dot_general(a.astype(MXU_DTYPE), b.astype(MXU_DTYPE), (((1,), (1,)), ((), ())),
                           preferred_element_type=F32)


def _mm_tn(a, b):
    return lax.dot_general(a.astype(MXU_DTYPE), b.astype(MXU_DTYPE), (((0,), (0,)), ((), ())),
                           preferred_element_type=F32)


def _softplus(z):
    return jnp.maximum(z, 0.0) + jnp.log1p(jnp.exp(-jnp.abs(z)))


def _log_sigmoid(z):
    return jnp.minimum(z, 0.0) - jnp.log1p(jnp.exp(-jnp.abs(z)))


def _layer_norm(v, g, b):
    mu = jnp.mean(v, axis=-1, keepdims=True)
    d = v - mu
    var = jnp.mean(d * d, axis=-1, keepdims=True)
    return d * lax.rsqrt(var + LN_EPS) * g + b


def _rg_kernel(x_ref, w_ref, cw_ref, cb_ref, wr_ref, br_ref, wi_ref, bi_ref, lam_ref, y_ref,
               tail_ref, h_ref, *, ts, width, taps):
    @pl.when(pl.program_id(1) == 0)
    def _():
        tail_ref[...] = jnp.zeros_like(tail_ref)
        h_ref[...] = jnp.zeros_like(h_ref)

    u = _mm(x_ref[...], w_ref[...])
    ext = jnp.concatenate([tail_ref[...], u], axis=0)
    c = cb_ref[...]
    for j in range(taps):
        off = 8 - (taps - 1) + j
        c = c + ext[off:off + ts] * cw_ref[j:j + 1, :]
    tail_ref[...] = u[ts - 8:]

    cm = c.astype(MXU_DTYPE)
    nblk = width // V7X_MXU_DIM
    zr = jnp.concatenate([_mm(cm[:, n * V7X_MXU_DIM:(n + 1) * V7X_MXU_DIM], wr_ref[n]) for n in range(nblk)], axis=1)
    zi = jnp.concatenate([_mm(cm[:, n * V7X_MXU_DIM:(n + 1) * V7X_MXU_DIM], wi_ref[n]) for n in range(nblk)], axis=1)
    r = jax.nn.sigmoid(zr + br_ref[...])
    i = jax.nn.sigmoid(zi + bi_ref[...])
    log_a = (-RG_C) * r * _softplus(-lam_ref[...])
    a = jnp.exp(log_a)
    b = jnp.sqrt(jnp.maximum(-jnp.tanh(log_a) * (a * a + 1.0), 0.0)) * (i * c)

    row = lax.broadcasted_iota(jnp.int32, (ts, width), 0)
    shift = 1
    while shift < ts:
        valid = row >= shift
        b = jnp.where(valid, a * pltpu.roll(b, shift, 0) + b, b)
        a = jnp.where(valid, a * pltpu.roll(a, shift, 0), a)
        shift *= 2
    h = b + a * h_ref[...]
    h_ref[...] = h[ts - 1:ts, :]
    y_ref[...] = h.astype(y_ref.dtype)


def _rg_branch(x, w, conv_w, conv_b, wr_bd, br, wi_bd, bi, lam, *, ts):
    B, S, D = x.shape
    width = w.shape[1]
    taps = conv_w.shape[0]
    nblk = width // V7X_MXU_DIM
    full = lambda shape: pl.BlockSpec(shape, lambda b, s: (0,) * len(shape))
    return pl.pallas_call(
        functools.partial(_rg_kernel, ts=ts, width=width, taps=taps),
        grid=(B, S // ts),
        in_specs=[
            pl.BlockSpec((None, ts, D), lambda b, s: (b, s, 0)),
            full((D, width)), full((taps, width)), full((1, width)),
            full((nblk, V7X_MXU_DIM, V7X_MXU_DIM)), full((1, width)),
            full((nblk, V7X_MXU_DIM, V7X_MXU_DIM)), full((1, width)), full((1, width)),
        ],
        out_specs=pl.BlockSpec((None, ts, width), lambda b, s: (b, s, 0)),
        out_shape=jax.ShapeDtypeStruct((B, S, width), MXU_DTYPE),
        scratch_shapes=[pltpu.VMEM((8, width), F32), pltpu.VMEM((1, width), F32)],
        compiler_params=_cparams("parallel", "arbitrary"),
        name="rg_lru",
    )(x, w, conv_w, conv_b, wr_bd, br, wi_bd, bi, lam)


def _cla_scratch(ts, heads, dk, dv):
    hk, hv = heads * dk, heads * dv
    return [
        pltpu.VMEM((ts, hk), F32),
        pltpu.VMEM((PAD + ts, hk), F32),
        pltpu.VMEM((PAD + ts, hk), F32),
        pltpu.VMEM((PAD + ts, hv), F32),
        pltpu.VMEM((ts, hk), MXU_DTYPE),
        pltpu.VMEM((ts, hk), MXU_DTYPE),
        pltpu.VMEM((ts, hv), F32),
        pltpu.VMEM((heads, dv, dk), F32),
    ]


def _cla_core(q, k, lg, v, scratch, *, ts, heads, dk, dv):
    q_s, k_s, g_s, v_s, qt_s, qg_s, o_s, st_ref = scratch
    hk = heads * dk
    nsub = CHUNK // SUB

    @pl.when(pl.program_id(1) == 0)
    def _():
        st_ref[...] = jnp.zeros_like(st_ref)
        k_s[0:PAD, :] = jnp.zeros((PAD, hk), F32)
        g_s[0:PAD, :] = jnp.zeros((PAD, hk), F32)
        v_s[0:PAD, :] = jnp.zeros((PAD, heads * dv), F32)

    row = lax.broadcasted_iota(jnp.int32, (ts, hk), 0)
    in_sub = row % SUB
    in_chunk = row % CHUNK
    p = lg
    g = lg
    shift = 1
    while shift < CHUNK:
        if shift < SUB:
            p = p + jnp.where(in_sub >= shift, pltpu.roll(p, shift, 0), 0.0)
        g = g + jnp.where(in_chunk >= shift, pltpu.roll(g, shift, 0), 0.0)
        shift *= 2
    unsafe = jnp.max(-p) > SAFE_EXP

    q_s[...] = q
    k_s[PAD:PAD + ts, :] = k
    g_s[PAD:PAD + ts, :] = g
    v_s[PAD:PAD + ts, :] = v
    qt_s[...] = (q * jnp.exp(p)).astype(MXU_DTYPE)
    qg_s[...] = (q * jnp.exp(g)).astype(MXU_DTYPE)

    ci = lax.broadcasted_iota(jnp.int32, (CHUNK, CHUNK), 0)
    si = lax.broadcasted_iota(jnp.int32, (CHUNK, CHUNK), 1)
    col_limit = jnp.where(unsafe, (ci // SUB) * SUB - 1, ci)
    keep = si <= col_limit

    def chunk_body(n, carry):
        r0 = pl.multiple_of(n * CHUNK, CHUNK)
        gc = g_s[pl.ds(PAD + r0, CHUNK), :]
        kc = k_s[pl.ds(PAD + r0, CHUNK), :]
        vc = v_s[pl.ds(PAD + r0, CHUNK), :].astype(MXU_DTYPE)
        qtc = qt_s[pl.ds(r0, CHUNK), :]
        qgc = qg_s[pl.ds(r0, CHUNK), :]
        g_last = gc[CHUNK - 1:CHUNK, :]
        k_dec = (kc * jnp.exp(g_last - gc)).astype(MXU_DTYPE)
        dec = jnp.exp(g_last)
        k_sub = []
        for i in range(nsub):
            ref = gc[SUB * i - 1:SUB * i, :] if i else jnp.zeros((1, hk), F32)
            k_sub.append((kc * jnp.exp(jnp.minimum(ref - gc, SAFE_EXP))).astype(MXU_DTYPE))
        for h in range(heads):
            ks = slice(h * dk, (h + 1) * dk)
            vs = slice(h * dv, (h + 1) * dv)
            st = st_ref[h]
            o = _mm_nt(qgc[:, ks], st)
            a = jnp.concatenate(
                [_mm_nt(qtc[SUB * i:SUB * (i + 1), ks], k_sub[i][:, ks]) for i in range(nsub)], axis=0)
            a = jnp.where(keep, a, 0.0)
            o = o + _mm(a, vc[:, vs])
            o_s[pl.ds(r0, CHUNK), vs] = o
            st_ref[h] = st * dec[:, ks] + _mm_tn(vc[:, vs], k_dec[:, ks])
        return carry

    lax.fori_loop(0, ts // CHUNK, chunk_body, 0)

    @pl.when(unsafe)
    def _():
        g_t = g_s[PAD:PAD + ts, :]
        q_t = q_s[...]
        for d in range(SUB):
            g_sh = g_s[PAD - d:PAD - d + ts, :]
            k_sh = k_s[PAD - d:PAD - d + ts, :]
            w = jnp.where(in_sub >= d, q_t * k_sh * jnp.exp(jnp.minimum(g_t - g_sh, 0.0)), 0.0)
            for h in range(heads):
                coef = jnp.sum(w[:, h * dk:(h + 1) * dk], axis=1, keepdims=True)
                v_sh = v_s[PAD - d:PAD - d + ts, h * dv:(h + 1) * dv]
                o_s[:, h * dv:(h + 1) * dv] += coef * v_sh

    return o_s[...]


def _gla_kernel(x_ref, w_ref, wa_ref, ba_ref, nw_ref, y_ref, *scratch, ts):
    hk, hv = GLA_HEADS * GLA_DK, GLA_HEADS * GLA_DV
    proj = _mm(x_ref[...], w_ref[...])
    q = proj[:, :hk] * (GLA_DK ** -0.5)
    k = proj[:, hk:2 * hk]
    v = proj[:, 2 * hk:2 * hk + hv]
    gate = proj[:, 2 * hk + hv:2 * hk + 2 * hv]
    a_low = proj[:, 2 * hk + 2 * hv:]
    lg = _log_sigmoid(_mm(a_low, wa_ref[...]) + ba_ref[...]) * (1.0 / GLA_GATE_NORMALIZER)
    o = _cla_core(q, k, lg, v, scratch, ts=ts, heads=GLA_HEADS, dk=GLA_DK, dv=GLA_DV)
    outs = []
    for h in range(GLA_HEADS):
        oh = o[:, h * GLA_DV:(h + 1) * GLA_DV]
        ms = jnp.mean(oh * oh, axis=-1, keepdims=True)
        outs.append(oh * lax.rsqrt(ms + RMS_EPS) * nw_ref[...])
    y_ref[...] = (jnp.concatenate(outs, axis=1) * jax.nn.silu(gate)).astype(y_ref.dtype)


def _hg_kernel(x_ref, w_ref, lbl_ref, nw_ref, y_ref, *scratch, ts, layer):
    hk, hv = HG_HEADS * HG_DK, HG_HEADS * HG_DV
    proj = _mm(x_ref[...], w_ref[...])
    lbl = lbl_ref[...]
    e = jnp.exp(lbl - jnp.max(lbl, axis=0, keepdims=True))
    lb_w = e / jnp.sum(e, axis=0, keepdims=True)
    lb = jnp.zeros((1, hk), F32)
    for l in range(1, layer + 1):
        lb = lb + lb_w[l:l + 1, :]
    z = proj[:, hk:2 * hk]
    forget = lb + (1.0 - lb) * jax.nn.sigmoid(z)
    lg = jnp.log(jnp.maximum(forget, HG_F_MIN))
    k = (1.0 - lb) * jax.nn.sigmoid(-z)
    q = jax.nn.silu(proj[:, :hk]) * (HG_DK ** -0.5)
    v = proj[:, 2 * hk:2 * hk + hv]
    gate = proj[:, 2 * hk + hv:]
    o = _cla_core(q, k, lg, v, scratch, ts=ts, heads=HG_HEADS, dk=HG_DK, dv=HG_DV)
    ms = jnp.mean(o * o, axis=-1, keepdims=True)
    y_ref[...] = (o * lax.rsqrt(ms + RMS_EPS) * nw_ref[...] * jax.nn.sigmoid(gate)).astype(y_ref.dtype)


def _gla_branch(x, w, wa, ba, nw, *, ts):
    B, S, D = x.shape
    hv = GLA_HEADS * GLA_DV
    full = lambda shape: pl.BlockSpec(shape, lambda b, s: (0,) * len(shape))
    return pl.pallas_call(
        functools.partial(_gla_kernel, ts=ts),
        grid=(B, S // ts),
        in_specs=[pl.BlockSpec((None, ts, D), lambda b, s: (b, s, 0)),
                  full(w.shape), full(wa.shape), full(ba.shape), full(nw.shape)],
        out_specs=pl.BlockSpec((None, ts, hv), lambda b, s: (b, s, 0)),
        out_shape=jax.ShapeDtypeStruct((B, S, hv), MXU_DTYPE),
        scratch_shapes=_cla_scratch(ts, GLA_HEADS, GLA_DK, GLA_DV),
        compiler_params=_cparams("parallel", "arbitrary"),
        name="gla",
    )(x, w, wa, ba, nw)


def _hg_branch(x, w, lb_logits, nw, *, ts, layer):
    B, S, D = x.shape
    hv = HG_HEADS * HG_DV
    full = lambda shape: pl.BlockSpec(shape, lambda b, s: (0,) * len(shape))
    return pl.pallas_call(
        functools.partial(_hg_kernel, ts=ts, layer=layer),
        grid=(B, S // ts),
        in_specs=[pl.BlockSpec((None, ts, D), lambda b, s: (b, s, 0)),
                  full(w.shape), full(lb_logits.shape), full(nw.shape)],
        out_specs=pl.BlockSpec((None, ts, hv), lambda b, s: (b, s, 0)),
        out_shape=jax.ShapeDtypeStruct((B, S, hv), MXU_DTYPE),
        scratch_shapes=_cla_scratch(ts, HG_HEADS, HG_DK, HG_DV),
        compiler_params=_cparams("parallel", "arbitrary"),
        name="hgrn2",
    )(x, w, lb_logits, nw)


def _merge_kernel(x_ref, yr_ref, yg_ref, yh_ref, wgate_ref, wbr_ref, wo_ref, g_ref, b_ref, *rest,
                  alpha, n_experts):
    if n_experts:
        rw_ref, out_ref, route_ref = rest
    else:
        (out_ref,) = rest
    x = x_ref[...]
    xm = x.astype(MXU_DTYPE)
    mixed = None
    for n, y_ref in enumerate((yr_ref, yg_ref, yh_ref)):
        term = jax.nn.sigmoid(_mm(xm, wgate_ref[n])) * _mm(y_ref[...], wbr_ref[n])
        mixed = term if mixed is None else mixed + term
    out = _layer_norm(alpha * x + _mm(mixed, wo_ref[...]), g_ref[...], b_ref[...])
    out_ref[...] = out
    if n_experts:
        logits = jnp.dot(out, rw_ref[...], preferred_element_type=F32, precision=lax.Precision.HIGHEST)
        lane = lax.broadcasted_iota(jnp.int32, logits.shape, 1)
        neg = jnp.float32(-jnp.inf)
        l1 = jnp.where(lane < n_experts, logits, neg)
        m1 = jnp.max(l1, axis=1, keepdims=True)
        i1 = jnp.min(jnp.where(l1 == m1, lane, V7X_LANES), axis=1, keepdims=True)
        l2 = jnp.where(lane == i1, neg, l1)
        m2 = jnp.max(l2, axis=1, keepdims=True)
        i2 = jnp.min(jnp.where(l2 == m2, lane, V7X_LANES), axis=1, keepdims=True)
        e2 = jnp.exp(m2 - m1)
        w1 = 1.0 / (1.0 + e2)
        w2 = e2 / (1.0 + e2)
        route = jnp.where(lane == 0, i1.astype(F32),
                          jnp.where(lane == 1, i2.astype(F32),
                                    jnp.where(lane == 2, w1, jnp.where(lane == 3, w2, 0.0))))
        route_ref[...] = route


def _merge(x2d, y_rg, y_gla, y_hg, w_gate, w_branch, w_out, ln_g, ln_b, router_w, *, alpha, tm):
    T, D = x2d.shape
    bw = y_rg.shape[1]
    n_experts = 0 if router_w is None else router_w.shape[1]
    full = lambda shape: pl.BlockSpec(shape, lambda i: (0,) * len(shape))
    row = lambda width: pl.BlockSpec((tm, width), lambda i: (i, 0))
    in_specs = [row(D), row(bw), row(bw), row(bw), full(w_gate.shape), full(w_branch.shape),
                full(w_out.shape), full((1, D)), full((1, D))]
    args = [x2d, y_rg, y_gla, y_hg, w_gate, w_branch, w_out, ln_g, ln_b]
    out_specs = row(D)
    out_shape = jax.ShapeDtypeStruct((T, D), F32)
    if n_experts:
        rw = jnp.pad(router_w, ((0, 0), (0, V7X_LANES - n_experts)))
        in_specs.append(full(rw.shape))
        args.append(rw)
        out_specs = (out_specs, row(V7X_LANES))
        out_shape = (out_shape, jax.ShapeDtypeStruct((T, V7X_LANES), F32))
    return pl.pallas_call(
        functools.partial(_merge_kernel, alpha=alpha, n_experts=n_experts),
        grid=(T // tm,),
        in_specs=in_specs,
        out_specs=out_specs,
        out_shape=out_shape,
        compiler_params=_cparams("parallel"),
        name="merge_router" if n_experts else "merge",
    )(*args)


def _ffn_kernel(x_ref, wg_ref, wu_ref, wd_ref, g_ref, b_ref, o_ref, xm_s, acc_s, *, alpha):
    f = pl.program_id(1)

    @pl.when(f == 0)
    def _():
        xm_s[...] = x_ref[...].astype(MXU_DTYPE)
        acc_s[...] = jnp.zeros_like(acc_s)

    xm = xm_s[...]
    hidden = jax.nn.silu(_mm(xm, wg_ref[...])) * _mm(xm, wu_ref[...])
    acc_s[...] += _mm(hidden, wd_ref[...])

    @pl.when(f == pl.num_programs(1) - 1)
    def _():
        o_ref[...] = _layer_norm(alpha * x_ref[...] + acc_s[...], g_ref[...], b_ref[...])


def _ffn(x2d, wg, wu, wd, ln_g, ln_b, *, alpha, tm, tf):
    T, D = x2d.shape
    F = wg.shape[1]
    return pl.pallas_call(
        functools.partial(_ffn_kernel, alpha=alpha),
        grid=(T // tm, F // tf),
        in_specs=[
            pl.BlockSpec((tm, D), lambda i, f: (i, 0)),
            pl.BlockSpec((D, tf), lambda i, f: (0, f)),
            pl.BlockSpec((D, tf), lambda i, f: (0, f)),
            pl.BlockSpec((tf, D), lambda i, f: (f, 0)),
            pl.BlockSpec((1, D), lambda i, f: (0, 0)),
            pl.BlockSpec((1, D), lambda i, f: (0, 0)),
        ],
        out_specs=pl.BlockSpec((tm, D), lambda i, f: (i, 0)),
        out_shape=jax.ShapeDtypeStruct((T, D), F32),
        scratch_shapes=[pltpu.VMEM((tm, D), MXU_DTYPE), pltpu.VMEM((tm, D), F32)],
        compiler_params=_cparams("parallel", "arbitrary"),
        name="ffn_swiglu",
    )(x2d, wg, wu, wd, ln_g, ln_b)


def _row_copy(src_hbm, src_row, dst, dst_row, sem):
    return pltpu.make_async_copy(src_hbm.at[pl.ds(src_row, 1), :], dst.at[pl.ds(dst_row, 1), :], sem)


def _moe_kernel(be_ref, nb_ref, tok_ref, tok_next_ref, x_hbm, wg_ref, wu_ref, wd_ref, o_ref,
                xg_s, sem, xm_s, acc_s, *, bm):
    i = pl.program_id(0)
    f = pl.program_id(1)
    n_used = nb_ref[0]
    slot = lax.rem(i, 2)

    def start_gather(t_ref, sl):
        def body(r, c):
            _row_copy(x_hbm, t_ref[0, 0, r], xg_s.at[sl], r, sem.at[sl]).start()
            return c
        lax.fori_loop(0, bm, body, 0)

    def wait_gather(sl):
        def body(r, c):
            _row_copy(x_hbm, 0, xg_s.at[sl], r, sem.at[sl]).wait()
            return c
        lax.fori_loop(0, bm, body, 0)

    @pl.when(f == 0)
    def _():
        @pl.when(i == 0)
        def _():
            start_gather(tok_ref, 0)

        @pl.when(i + 1 < n_used)
        def _():
            start_gather(tok_next_ref, 1 - slot)

        @pl.when(i < n_used)
        def _():
            wait_gather(slot)
            xm_s[...] = xg_s[slot].astype(MXU_DTYPE)
            acc_s[...] = jnp.zeros_like(acc_s)

    @pl.when(i < n_used)
    def _():
        xm = xm_s[...]
        hidden = jax.nn.silu(_mm(xm, wg_ref[...])) * _mm(xm, wu_ref[...])
        acc_s[...] += _mm(hidden, wd_ref[...])

    last = f == pl.num_programs(1) - 1

    @pl.when(jnp.logical_and(last, i < n_used))
    def _():
        o_ref[...] = acc_s[...]

    @pl.when(jnp.logical_and(last, i >= n_used))
    def _():
        o_ref[...] = jnp.zeros_like(o_ref)


def _moe_experts(x2d, block_expert, n_used, buf_tok, wg, wu, wd, *, bm, tf):
    T, D = x2d.shape
    E, _, F = wg.shape
    nb = buf_tok.shape[0]
    nf = F // tf

    def w_idx(i, f, be, nu):
        return jnp.where(i < nu[0], f, nf - 1)

    grid_spec = pltpu.PrefetchScalarGridSpec(
        num_scalar_prefetch=2,
        grid=(nb, nf),
        in_specs=[
            pl.BlockSpec((1, 1, bm), lambda i, f, be, nu: (i, 0, 0), memory_space=pltpu.SMEM),
            pl.BlockSpec((1, 1, bm), lambda i, f, be, nu: (jnp.minimum(i + 1, nb - 1), 0, 0),
                         memory_space=pltpu.SMEM),
            pl.BlockSpec(memory_space=pl.ANY),
            pl.BlockSpec((None, D, tf), lambda i, f, be, nu: (be[i], 0, w_idx(i, f, be, nu))),
            pl.BlockSpec((None, D, tf), lambda i, f, be, nu: (be[i], 0, w_idx(i, f, be, nu))),
            pl.BlockSpec((None, tf, D), lambda i, f, be, nu: (be[i], w_idx(i, f, be, nu), 0)),
        ],
        out_specs=pl.BlockSpec((bm, D), lambda i, f, be, nu: (i, 0)),
        scratch_shapes=[
            pltpu.VMEM((2, bm, D), F32),
            pltpu.SemaphoreType.DMA((2,)),
            pltpu.VMEM((bm, D), MXU_DTYPE),
            pltpu.VMEM((bm, D), F32),
        ],
    )
    return pl.pallas_call(
        functools.partial(_moe_kernel, bm=bm),
        grid_spec=grid_spec,
        out_shape=jax.ShapeDtypeStruct((nb * bm, D), F32),
        compiler_params=_cparams("arbitrary", "arbitrary"),
        name="moe_experts",
    )(block_expert, n_used, buf_tok, buf_tok, x2d, wg, wu, wd)


def _combine_kernel(pos_ref, pos_next_ref, ys_hbm, x_ref, route_ref, g_ref, b_ref, o_ref, yg_s, sem,
                    *, tm, alpha):
    i = pl.program_id(0)
    n = pl.num_programs(0)
    slot = lax.rem(i, 2)
    rows = TOP_K * tm

    def start_gather(p_ref, sl):
        def body(r, c):
            _row_copy(ys_hbm, p_ref[0, 0, r], yg_s.at[sl], r, sem.at[sl]).start()
            return c
        lax.fori_loop(0, rows, body, 0)

    def wait_gather(sl):
        def body(r, c):
            _row_copy(ys_hbm, 0, yg_s.at[sl], r, sem.at[sl]).wait()
            return c
        lax.fori_loop(0, rows, body, 0)

    @pl.when(i == 0)
    def _():
        start_gather(pos_ref, 0)

    @pl.when(i + 1 < n)
    def _():
        start_gather(pos_next_ref, 1 - slot)

    wait_gather(slot)
    route = route_ref[...]
    y = alpha * x_ref[...]
    for k in range(TOP_K):
        y = y + route[:, TOP_K + k:TOP_K + k + 1] * yg_s[slot, k * tm:(k + 1) * tm, :]
    o_ref[...] = _layer_norm(y, g_ref[...], b_ref[...])


def _moe_combine(ys, pos, x2d, route, ln_g, ln_b, *, alpha, tm):
    T, D = x2d.shape
    n = T // tm
    return pl.pallas_call(
        functools.partial(_combine_kernel, tm=tm, alpha=alpha),
        grid=(n,),
        in_specs=[
            pl.BlockSpec((1, 1, TOP_K * tm), lambda i: (i, 0, 0), memory_space=pltpu.SMEM),
            pl.BlockSpec((1, 1, TOP_K * tm), lambda i: (jnp.minimum(i + 1, n - 1), 0, 0),
                         memory_space=pltpu.SMEM),
            pl.BlockSpec(memory_space=pl.ANY),
            pl.BlockSpec((tm, D), lambda i: (i, 0)),
            pl.BlockSpec((tm, V7X_LANES), lambda i: (i, 0)),
            pl.BlockSpec((1, D), lambda i: (0, 0)),
            pl.BlockSpec((1, D), lambda i: (0, 0)),
        ],
        out_specs=pl.BlockSpec((tm, D), lambda i: (i, 0)),
        out_shape=jax.ShapeDtypeStruct((T, D), F32),
        scratch_shapes=[pltpu.VMEM((2, TOP_K * tm, D), F32), pltpu.SemaphoreType.DMA((2,))],
        compiler_params=_cparams("arbitrary"),
        name="moe_combine",
    )(pos, pos, ys, x2d, route, ln_g, ln_b)


def _moe_layer(x2d, route, wg, wu, wd, ln_g, ln_b, *, alpha, bm, tf, tm):
    T, D = x2d.shape
    E = wg.shape[0]
    tk = T * TOP_K
    nb = tk // bm + E
    expert = route[:, :TOP_K].astype(jnp.int32).reshape(tk)
    onehot = (expert[:, None] == jnp.arange(E, dtype=jnp.int32)[None, :]).astype(jnp.int32)
    csum = jnp.cumsum(onehot, axis=0)
    counts = csum[-1]
    rank = jnp.sum((csum - onehot) * onehot, axis=1)
    padded = (counts + bm - 1) // bm * bm
    pad_end = jnp.cumsum(padded)
    pad_start = pad_end - padded
    dest = jnp.sum(pad_start[None, :] * onehot, axis=1) + rank
    token = jnp.arange(tk, dtype=jnp.int32) // TOP_K
    buf_tok = jnp.zeros((nb * bm,), jnp.int32).at[dest].set(token)
    block_start = jnp.arange(nb, dtype=jnp.int32) * bm
    block_expert = jnp.minimum(jnp.searchsorted(pad_end, block_start, side="right"), E - 1).astype(jnp.int32)
    n_used = (pad_end[-1] // bm).astype(jnp.int32).reshape(1)
    ys = _moe_experts(x2d, block_expert, n_used, buf_tok.reshape(nb, 1, bm), wg, wu, wd, bm=bm, tf=tf)
    pos = dest.reshape(T // tm, tm, TOP_K).transpose(0, 2, 1).reshape(T // tm, 1, TOP_K * tm)
    return _moe_combine(ys, pos, x2d, route, ln_g, ln_b, alpha=alpha, tm=tm)


def _block_diag(w):
    g, n, _ = w.shape
    per = V7X_MXU_DIM // n
    w = w.reshape(g // per, per, n, n)
    eye = jnp.eye(per, dtype=w.dtype)
    return jnp.einsum("apij,pq->apiqj", w, eye).reshape(g // per, V7X_MXU_DIM, V7X_MXU_DIM)


def _pick(n, pref):
    t = min(n, pref)
    while n % t:
        t //= 2
    return t


def kernel(x, w_in, conv_w, conv_b, rg_wr, rg_br, rg_wi, rg_bi, rg_lambda, gla_wa, gla_ba, gla_norm_w, hg_lb_logits, hg_norm_w, w_branch, w_out, ln1_g, ln1_b, ln2_g, ln2_b, ffn_wg, ffn_wu, ffn_wd, router_w, moe_wg, moe_wu, moe_wd):
    B, S, D = x.shape
    depth = w_in.shape[0]
    T = B * S
    alpha = (2 * depth) ** 0.25
    rg_w = conv_w.shape[2]
    g_hk, g_hv = GLA_HEADS * GLA_DK, GLA_HEADS * GLA_DV
    rank = gla_wa.shape[1]
    h_hk, h_hv = HG_HEADS * HG_DK, HG_HEADS * HG_DV
    n_br = w_branch.shape[1]
    sizes = (rg_w, g_hk, g_hk, g_hv, rank, g_hv, h_hk, h_hk, h_hv, h_hv, n_br * D)
    offs = [0]
    for sz in sizes:
        offs.append(offs[-1] + sz)
    assert offs[-1] == w_in.shape[2]

    ts_rg = _pick(S, 512)
    ts_cla = _pick(S, 256)
    tm = _pick(T, 512)
    row2 = lambda v: v.reshape(1, -1)
    cast = lambda w: w.astype(MXU_DTYPE)

    x2d = x.reshape(T, D)
    for l in range(depth):
        wl = w_in[l]
        w_rg = cast(wl[:, offs[0]:offs[1]])
        a_cols = jnp.pad(wl[:, offs[4]:offs[5]], ((0, 0), (0, V7X_LANES - rank)))
        w_gla = cast(jnp.concatenate([wl[:, offs[1]:offs[4]], wl[:, offs[5]:offs[6]], a_cols], axis=1))
        w_hg = cast(wl[:, offs[6]:offs[10]])
        w_gate = cast(wl[:, offs[10]:offs[11]].reshape(D, n_br, D).transpose(1, 0, 2))
        wa = cast(jnp.pad(gla_wa[l], ((0, V7X_LANES - rank), (0, 0))))
        x3d = x2d.reshape(B, S, D)

        y_rg = _rg_branch(x3d, w_rg, conv_w[l], row2(conv_b[l]), cast(_block_diag(rg_wr[l])), row2(rg_br[l]),
                          cast(_block_diag(rg_wi[l])), row2(rg_bi[l]), row2(rg_lambda[l]), ts=ts_rg)
        y_gla = _gla_branch(x3d, w_gla, wa, row2(gla_ba[l]), row2(gla_norm_w[l]), ts=ts_cla)
        y_hg = _hg_branch(x3d, w_hg, hg_lb_logits, row2(hg_norm_w[l]), ts=ts_cla, layer=l)

        is_moe = l % 2 == 1
        j = l // 2
        merged = _merge(x2d, y_rg.reshape(T, -1), y_gla.reshape(T, -1), y_hg.reshape(T, -1), w_gate,
                        cast(w_branch[l]), cast(w_out[l]), row2(ln1_g[l]), row2(ln1_b[l]),
                        router_w[j] if is_moe else None, alpha=alpha, tm=tm)
        if is_moe:
            x1, route = merged
            x2d = _moe_layer(x1, route, cast(moe_wg[j]), cast(moe_wu[j]), cast(moe_wd[j]),
                             row2(ln2_g[l]), row2(ln2_b[l]), alpha=alpha,
                             bm=_pick(T * TOP_K, 512), tf=_pick(moe_wg.shape[3], 512), tm=_pick(T, 256))
        else:
            x2d = _ffn(merged, cast(ffn_wg[j]), cast(ffn_wu[j]), cast(ffn_wd[j]), row2(ln2_g[l]), row2(ln2_b[l]),
                       alpha=alpha, tm=_pick(T, 1024), tf=_pick(ffn_wg.shape[2], 256))
    return x2d.reshape(B, S, D)
```

```python
import functools

import jax
import jax.numpy as jnp
from jax import lax
from jax.experimental import pallas as pl
from jax.experimental.pallas import tpu as pltpu

F32 = jnp.float32
MXU_DTYPE = jnp.bfloat16

RG_C = 8.0
GLA_HEADS, GLA_DK, GLA_DV = 4, 64, 128
GLA_GATE_NORMALIZER = 16.0
HG_HEADS, HG_DK, HG_DV = 4, 128, 128
HG_F_MIN = 1e-30
TOP_K = 2
LN_EPS = 1e-5
RMS_EPS = 1e-6

CHUNK = 64
SUB = 16
PAD = SUB
SAFE_EXP = 60.0

V7X_LANES = 128
V7X_MXU_DIM = 256
V7X_VMEM_BYTES = 64 * 1024 * 1024
VMEM_LIMIT = V7X_VMEM_BYTES * 7 // 8


def _cparams(*sem):
    return pltpu.CompilerParams(dimension_semantics=sem, vmem_limit_bytes=VMEM_LIMIT)


def _mm(a, b):
    return jnp.dot(a.astype(MXU_DTYPE), b.astype(MXU_DTYPE), preferred_element_type=F32)


def _mm_nt(a, b):
    return lax.dot_general(a.astype(MXU_DTYPE), b.astype(MXU_DTYPE), (((1,), (1,)), ((), ())),
                           preferred_element_type=F32)


def _mm_tn(a, b):
    return lax.dot_general(a.astype(MXU_DTYPE), b.astype(MXU_DTYPE), (((0,), (0,)), ((), ())),
                           preferred_element_type=F32)


def _softplus(z):
    return jnp.maximum(z, 0.0) + jnp.log1p(jnp.exp(-jnp.abs(z)))


def _log_sigmoid(z):
    return jnp.minimum(z, 0.0) - jnp.log1p(jnp.exp(-jnp.abs(z)))


def _store_row_tiles(ref, val):
    n, d = val.shape
    per_row = d // V7X_LANES
    for j in range(per_row):
        ref[pl.ds(j, n, stride=per_row), :] = val[:, j * V7X_LANES:(j + 1) * V7X_LANES]


def _load_row_tiles(ref, first_row, n, d):
    per_row = d // V7X_LANES
    return jnp.concatenate(
        [ref[pl.ds(first_row * per_row + j, n, stride=per_row), :] for j in range(per_row)], axis=1)


def _layer_norm(v, g, b):
    mu = jnp.mean(v, axis=-1, keepdims=True)
    d = v - mu
    var = jnp.mean(d * d, axis=-1, keepdims=True)
    return d * lax.rsqrt(var + LN_EPS) * g + b


def _rg_kernel(x_ref, w_ref, cw_ref, cb_ref, wr_ref, br_ref, wi_ref, bi_ref, lam_ref, y_ref,
               tail_ref, h_ref, *, ts, width, taps):
    @pl.when(pl.program_id(1) == 0)
    def _():
        tail_ref[...] = jnp.zeros_like(tail_ref)
        h_ref[...] = jnp.zeros_like(h_ref)

    u = _mm(x_ref[...], w_ref[...])
    ext = jnp.concatenate([tail_ref[...], u], axis=0)
    c = cb_ref[...]
    for j in range(taps):
        off = 8 - (taps - 1) + j
        c = c + ext[off:off + ts] * cw_ref[j:j + 1, :]
    tail_ref[...] = u[ts - 8:]

    cm = c.astype(MXU_DTYPE)
    nblk = width // V7X_MXU_DIM
    zr = jnp.concatenate([_mm(cm[:, n * V7X_MXU_DIM:(n + 1) * V7X_MXU_DIM], wr_ref[n]) for n in range(nblk)], axis=1)
    zi = jnp.concatenate([_mm(cm[:, n * V7X_MXU_DIM:(n + 1) * V7X_MXU_DIM], wi_ref[n]) for n in range(nblk)], axis=1)
    r = jax.nn.sigmoid(zr + br_ref[...])
    i = jax.nn.sigmoid(zi + bi_ref[...])
    log_a = (-RG_C) * r * _softplus(-lam_ref[...])
    a = jnp.exp(log_a)
    b = jnp.sqrt(jnp.maximum(-jnp.tanh(log_a) * (a * a + 1.0), 0.0)) * (i * c)

    row = lax.broadcasted_iota(jnp.int32, (ts, width), 0)
    shift = 1
    while shift < ts:
        valid = row >= shift
        b = jnp.where(valid, a * pltpu.roll(b, shift, 0) + b, b)
        a = jnp.where(valid, a * pltpu.roll(a, shift, 0), a)
        shift *= 2
    h = b + a * h_ref[...]
    h_ref[...] = h[ts - 1:ts, :]
    y_ref[...] = h.astype(y_ref.dtype)


def _rg_branch(x, w, conv_w, conv_b, wr_bd, br, wi_bd, bi, lam, *, ts):
    B, S, D = x.shape
    width = w.shape[1]
    taps = conv_w.shape[0]
    nblk = width // V7X_MXU_DIM
    full = lambda shape: pl.BlockSpec(shape, lambda b, s: (0,) * len(shape))
    return pl.pallas_call(
        functools.partial(_rg_kernel, ts=ts, width=width, taps=taps),
        grid=(B, S // ts),
        in_specs=[
            pl.BlockSpec((None, ts, D), lambda b, s: (b, s, 0)),
            full((D, width)), full((taps, width)), full((1, width)),
            full((nblk, V7X_MXU_DIM, V7X_MXU_DIM)), full((1, width)),
            full((nblk, V7X_MXU_DIM, V7X_MXU_DIM)), full((1, width)), full((1, width)),
        ],
        out_specs=pl.BlockSpec((None, ts, width), lambda b, s: (b, s, 0)),
        out_shape=jax.ShapeDtypeStruct((B, S, width), MXU_DTYPE),
        scratch_shapes=[pltpu.VMEM((8, width), F32), pltpu.VMEM((1, width), F32)],
        compiler_params=_cparams("parallel", "arbitrary"),
        name="rg_lru",
    )(x, w, conv_w, conv_b, wr_bd, br, wi_bd, bi, lam)


def _cla_scratch(ts, heads, dk, dv):
    hk, hv = heads * dk, heads * dv
    return [
        pltpu.VMEM((ts, hk), F32),
        pltpu.VMEM((PAD + ts, hk), F32),
        pltpu.VMEM((PAD + ts, hk), F32),
        pltpu.VMEM((PAD + ts, hv), F32),
        pltpu.VMEM((ts, hk), MXU_DTYPE),
        pltpu.VMEM((ts, hk), MXU_DTYPE),
        pltpu.VMEM((ts, hv), F32),
        pltpu.VMEM((heads, dv, dk), F32),
    ]


def _cla_core(q, k, lg, v, scratch, *, ts, heads, dk, dv):
    q_s, k_s, g_s, v_s, qt_s, qg_s, o_s, st_ref = scratch
    hk = heads * dk
    nsub = CHUNK // SUB

    @pl.when(pl.program_id(1) == 0)
    def _():
        st_ref[...] = jnp.zeros_like(st_ref)
        k_s[0:PAD, :] = jnp.zeros((PAD, hk), F32)
        g_s[0:PAD, :] = jnp.zeros((PAD, hk), F32)
        v_s[0:PAD, :] = jnp.zeros((PAD, heads * dv), F32)

    row = lax.broadcasted_iota(jnp.int32, (ts, hk), 0)
    in_sub = row % SUB
    in_chunk = row % CHUNK
    p = lg
    g = lg
    shift = 1
    while shift < CHUNK:
        if shift < SUB:
            p = p + jnp.where(in_sub >= shift, pltpu.roll(p, shift, 0), 0.0)
        g = g + jnp.where(in_chunk >= shift, pltpu.roll(g, shift, 0), 0.0)
        shift *= 2
    unsafe = jnp.max(-p) > SAFE_EXP

    q_s[...] = q
    k_s[PAD:PAD + ts, :] = k
    g_s[PAD:PAD + ts, :] = g
    v_s[PAD:PAD + ts, :] = v
    qt_s[...] = (q * jnp.exp(p)).astype(MXU_DTYPE)
    qg_s[...] = (q * jnp.exp(g)).astype(MXU_DTYPE)

    ci = lax.broadcasted_iota(jnp.int32, (CHUNK, CHUNK), 0)
    si = lax.broadcasted_iota(jnp.int32, (CHUNK, CHUNK), 1)
    col_limit = jnp.where(unsafe, (ci // SUB) * SUB - 1, ci)
    keep = si <= col_limit

    def chunk_body(n, carry):
        r0 = pl.multiple_of(n * CHUNK, CHUNK)
        gc = g_s[pl.ds(PAD + r0, CHUNK), :]
        kc = k_s[pl.ds(PAD + r0, CHUNK), :]
        vc = v_s[pl.ds(PAD + r0, CHUNK), :].astype(MXU_DTYPE)
        qtc = qt_s[pl.ds(r0, CHUNK), :]
        qgc = qg_s[pl.ds(r0, CHUNK), :]
        g_last = gc[CHUNK - 1:CHUNK, :]
        k_dec = (kc * jnp.exp(g_last - gc)).astype(MXU_DTYPE)
        dec = jnp.exp(g_last)
        k_sub = []
        for i in range(nsub):
            ref = gc[SUB * i - 1:SUB * i, :] if i else jnp.zeros((1, hk), F32)
            k_sub.append((kc * jnp.exp(jnp.minimum(ref - gc, SAFE_EXP))).astype(MXU_DTYPE))
        for h in range(heads):
            ks = slice(h * dk, (h + 1) * dk)
            vs = slice(h * dv, (h + 1) * dv)
            st = st_ref[h]
            o = _mm_nt(qgc[:, ks], st)
            a = jnp.concatenate(
                [_mm_nt(qtc[SUB * i:SUB * (i + 1), ks], k_sub[i][:, ks]) for i in range(nsub)], axis=0)
            a = jnp.where(keep, a, 0.0)
            o = o + _mm(a, vc[:, vs])
            o_s[pl.ds(r0, CHUNK), vs] = o
            st_ref[h] = st * dec[:, ks] + _mm_tn(vc[:, vs], k_dec[:, ks])
        return carry

    lax.fori_loop(0, ts // CHUNK, chunk_body, 0)

    @pl.when(unsafe)
    def _():
        g_t = g_s[PAD:PAD + ts, :]
        q_t = q_s[...]
        for d in range(SUB):
            g_sh = g_s[PAD - d:PAD - d + ts, :]
            k_sh = k_s[PAD - d:PAD - d + ts, :]
            w = jnp.where(in_sub >= d, q_t * k_sh * jnp.exp(jnp.minimum(g_t - g_sh, 0.0)), 0.0)
            for h in range(heads):
                coef = jnp.sum(w[:, h * dk:(h + 1) * dk], axis=1, keepdims=True)
                v_sh = v_s[PAD - d:PAD - d + ts, h * dv:(h + 1) * dv]
                o_s[:, h * dv:(h + 1) * dv] += coef * v_sh

    return o_s[...]


def _gla_kernel(x_ref, w_ref, wa_ref, ba_ref, nw_ref, y_ref, *scratch, ts):
    hk, hv = GLA_HEADS * GLA_DK, GLA_HEADS * GLA_DV
    proj = _mm(x_ref[...], w_ref[...])
    q = proj[:, :hk] * (GLA_DK ** -0.5)
    k = proj[:, hk:2 * hk]
    v = proj[:, 2 * hk:2 * hk + hv]
    gate = proj[:, 2 * hk + hv:2 * hk + 2 * hv]
    a_low = proj[:, 2 * hk + 2 * hv:]
    lg = _log_sigmoid(_mm(a_low, wa_ref[...]) + ba_ref[...]) * (1.0 / GLA_GATE_NORMALIZER)
    o = _cla_core(q, k, lg, v, scratch, ts=ts, heads=GLA_HEADS, dk=GLA_DK, dv=GLA_DV)
    outs = []
    for h in range(GLA_HEADS):
        oh = o[:, h * GLA_DV:(h + 1) * GLA_DV]
        ms = jnp.mean(oh * oh, axis=-1, keepdims=True)
        outs.append(oh * lax.rsqrt(ms + RMS_EPS) * nw_ref[...])
    y_ref[...] = (jnp.concatenate(outs, axis=1) * jax.nn.silu(gate)).astype(y_ref.dtype)


def _hg_kernel(x_ref, w_ref, lbl_ref, nw_ref, y_ref, *scratch, ts, layer):
    hk, hv = HG_HEADS * HG_DK, HG_HEADS * HG_DV
    proj = _mm(x_ref[...], w_ref[...])
    lbl = lbl_ref[...]
    e = jnp.exp(lbl - jnp.max(lbl, axis=0, keepdims=True))
    lb_w = e / jnp.sum(e, axis=0, keepdims=True)
    lb = jnp.zeros((1, hk), F32)
    for l in range(1, layer + 1):
        lb = lb + lb_w[l:l + 1, :]
    z = proj[:, hk:2 * hk]
    forget = lb + (1.0 - lb) * jax.nn.sigmoid(z)
    lg = jnp.log(jnp.maximum(forget, HG_F_MIN))
    k = (1.0 - lb) * jax.nn.sigmoid(-z)
    q = jax.nn.silu(proj[:, :hk]) * (HG_DK ** -0.5)
    v = proj[:, 2 * hk:2 * hk + hv]
    gate = proj[:, 2 * hk + hv:]
    o = _cla_core(q, k, lg, v, scratch, ts=ts, heads=HG_HEADS, dk=HG_DK, dv=HG_DV)
    ms = jnp.mean(o * o, axis=-1, keepdims=True)
    y_ref[...] = (o * lax.rsqrt(ms + RMS_EPS) * nw_ref[...] * jax.nn.sigmoid(gate)).astype(y_ref.dtype)


def _gla_branch(x, w, wa, ba, nw, *, ts):
    B, S, D = x.shape
    hv = GLA_HEADS * GLA_DV
    full = lambda shape: pl.BlockSpec(shape, lambda b, s: (0,) * len(shape))
    return pl.pallas_call(
        functools.partial(_gla_kernel, ts=ts),
        grid=(B, S // ts),
        in_specs=[pl.BlockSpec((None, ts, D), lambda b, s: (b, s, 0)),
                  full(w.shape), full(wa.shape), full(ba.shape), full(nw.shape)],
        out_specs=pl.BlockSpec((None, ts, hv), lambda b, s: (b, s, 0)),
        out_shape=jax.ShapeDtypeStruct((B, S, hv), MXU_DTYPE),
        scratch_shapes=_cla_scratch(ts, GLA_HEADS, GLA_DK, GLA_DV),
        compiler_params=_cparams("parallel", "arbitrary"),
        name="gla",
    )(x, w, wa, ba, nw)


def _hg_branch(x, w, lb_logits, nw, *, ts, layer):
    B, S, D = x.shape
    hv = HG_HEADS * HG_DV
    full = lambda shape: pl.BlockSpec(shape, lambda b, s: (0,) * len(shape))
    return pl.pallas_call(
        functools.partial(_hg_kernel, ts=ts, layer=layer),
        grid=(B, S // ts),
        in_specs=[pl.BlockSpec((None, ts, D), lambda b, s: (b, s, 0)),
                  full(w.shape), full(lb_logits.shape), full(nw.shape)],
        out_specs=pl.BlockSpec((None, ts, hv), lambda b, s: (b, s, 0)),
        out_shape=jax.ShapeDtypeStruct((B, S, hv), MXU_DTYPE),
        scratch_shapes=_cla_scratch(ts, HG_HEADS, HG_DK, HG_DV),
        compiler_params=_cparams("parallel", "arbitrary"),
        name="hgrn2",
    )(x, w, lb_logits, nw)


def _merge_kernel(x_ref, yr_ref, yg_ref, yh_ref, wgate_ref, wbr_ref, wo_ref, g_ref, b_ref, *rest,
                  alpha, n_experts):
    if n_experts:
        rw_ref, out_ref, route_ref, out_rows_ref = rest
    else:
        (out_ref,) = rest
    x = x_ref[...]
    xm = x.astype(MXU_DTYPE)
    mixed = None
    for n, y_ref in enumerate((yr_ref, yg_ref, yh_ref)):
        term = jax.nn.sigmoid(_mm(xm, wgate_ref[n])) * _mm(y_ref[...], wbr_ref[n])
        mixed = term if mixed is None else mixed + term
    out = _layer_norm(alpha * x + _mm(mixed, wo_ref[...]), g_ref[...], b_ref[...])
    out_ref[...] = out
    if n_experts:
        _store_row_tiles(out_rows_ref, out)
        x_hi = out.astype(MXU_DTYPE)
        x_lo = out - x_hi.astype(F32)
        rw = rw_ref[...]
        p_hi = _mm(x_hi, rw)
        logits = p_hi[:, :V7X_LANES] + p_hi[:, V7X_LANES:] + _mm(x_lo, rw[:, :V7X_LANES])
        lane = lax.broadcasted_iota(jnp.int32, logits.shape, 1)
        neg = jnp.float32(-jnp.inf)
        l1 = jnp.where(lane < n_experts, logits, neg)
        m1 = jnp.max(l1, axis=1, keepdims=True)
        i1 = jnp.min(jnp.where(l1 == m1, lane, V7X_LANES), axis=1, keepdims=True)
        l2 = jnp.where(lane == i1, neg, l1)
        m2 = jnp.max(l2, axis=1, keepdims=True)
        i2 = jnp.min(jnp.where(l2 == m2, lane, V7X_LANES), axis=1, keepdims=True)
        e2 = jnp.exp(m2 - m1)
        w1 = 1.0 / (1.0 + e2)
        w2 = e2 / (1.0 + e2)
        route = jnp.where(lane == 0, i1.astype(F32),
                          jnp.where(lane == 1, i2.astype(F32),
                                    jnp.where(lane == 2, w1, jnp.where(lane == 3, w2, 0.0))))
        route_ref[...] = route


def _merge(x2d, y_rg, y_gla, y_hg, w_gate, w_branch, w_out, ln_g, ln_b, router_w, *, alpha, tm):
    T, D = x2d.shape
    bw = y_rg.shape[1]
    n_experts = 0 if router_w is None else router_w.shape[1]
    full = lambda shape: pl.BlockSpec(shape, lambda i: (0,) * len(shape))
    row = lambda width: pl.BlockSpec((tm, width), lambda i: (i, 0))
    in_specs = [row(D), row(bw), row(bw), row(bw), full(w_gate.shape), full(w_branch.shape),
                full(w_out.shape), full((1, D)), full((1, D))]
    args = [x2d, y_rg, y_gla, y_hg, w_gate, w_branch, w_out, ln_g, ln_b]
    out_specs = row(D)
    out_shape = jax.ShapeDtypeStruct((T, D), F32)
    if n_experts:
        rw = jnp.pad(router_w, ((0, 0), (0, V7X_LANES - n_experts)))
        rw_hi = rw.astype(MXU_DTYPE)
        rw = jnp.concatenate([rw_hi, (rw - rw_hi.astype(F32)).astype(MXU_DTYPE)], axis=1)
        in_specs.append(full(rw.shape))
        args.append(rw)
        per_row = D // V7X_LANES
        out_specs = (out_specs, row(V7X_LANES), pl.BlockSpec((tm * per_row, V7X_LANES), lambda i: (i, 0)))
        out_shape = (out_shape, jax.ShapeDtypeStruct((T, V7X_LANES), F32),
                     jax.ShapeDtypeStruct((T * per_row, V7X_LANES), F32))
    return pl.pallas_call(
        functools.partial(_merge_kernel, alpha=alpha, n_experts=n_experts),
        grid=(T // tm,),
        in_specs=in_specs,
        out_specs=out_specs,
        out_shape=out_shape,
        compiler_params=_cparams("parallel"),
        name="merge_router" if n_experts else "merge",
    )(*args)


def _ffn_kernel(x_ref, wg_ref, wu_ref, wd_ref, g_ref, b_ref, o_ref, xm_s, acc_s, *, alpha):
    f = pl.program_id(1)

    @pl.when(f == 0)
    def _():
        xm_s[...] = x_ref[...].astype(MXU_DTYPE)
        acc_s[...] = jnp.zeros_like(acc_s)

    xm = xm_s[...]
    hidden = jax.nn.silu(_mm(xm, wg_ref[...])) * _mm(xm, wu_ref[...])
    acc_s[...] += _mm(hidden, wd_ref[...])

    @pl.when(f == pl.num_programs(1) - 1)
    def _():
        o_ref[...] = _layer_norm(alpha * x_ref[...] + acc_s[...], g_ref[...], b_ref[...])


def _ffn(x2d, wg, wu, wd, ln_g, ln_b, *, alpha, tm, tf):
    T, D = x2d.shape
    F = wg.shape[1]
    return pl.pallas_call(
        functools.partial(_ffn_kernel, alpha=alpha),
        grid=(T // tm, F // tf),
        in_specs=[
            pl.BlockSpec((tm, D), lambda i, f: (i, 0)),
            pl.BlockSpec((D, tf), lambda i, f: (0, f)),
            pl.BlockSpec((D, tf), lambda i, f: (0, f)),
            pl.BlockSpec((tf, D), lambda i, f: (f, 0)),
            pl.BlockSpec((1, D), lambda i, f: (0, 0)),
            pl.BlockSpec((1, D), lambda i, f: (0, 0)),
        ],
        out_specs=pl.BlockSpec((tm, D), lambda i, f: (i, 0)),
        out_shape=jax.ShapeDtypeStruct((T, D), F32),
        scratch_shapes=[pltpu.VMEM((tm, D), MXU_DTYPE), pltpu.VMEM((tm, D), F32)],
        compiler_params=_cparams("parallel", "arbitrary"),
        name="ffn_swiglu",
    )(x2d, wg, wu, wd, ln_g, ln_b)


def _row_copy(src_hbm, src_row, dst, dst_row, sem):
    per_row = src_hbm.shape[1]
    return pltpu.make_async_copy(src_hbm.at[src_row], dst.at[pl.ds(dst_row * per_row, per_row), :], sem)


def _moe_kernel(be_ref, nb_ref, tok_ref, tok_next_ref, x_hbm, wg_ref, wu_ref, wd_ref, o_ref,
                xg_s, sem, xm_s, acc_s, *, bm, per_step):
    i = pl.program_id(0)
    f = pl.program_id(1)
    nf = pl.num_programs(1)
    n_used = nb_ref[0]
    slot = lax.rem(i, 2)
    d = xm_s.shape[1]

    def start_rows(t_ref, sl, step):
        for j in range(per_step):
            r = step * per_step + j
            _row_copy(x_hbm, t_ref[0, 0, jnp.minimum(r, bm - 1)], xg_s.at[sl], r, sem.at[sl]).start()

    @pl.when(jnp.logical_and(f == 0, i == 0))
    def _():
        def body(step, c):
            start_rows(tok_ref, 0, step)
            return c
        lax.fori_loop(0, nf, body, 0)

    @pl.when(jnp.logical_and(f == 0, i <= n_used))
    def _():
        for r in range(xg_s.shape[1] // x_hbm.shape[1]):
            _row_copy(x_hbm, 0, xg_s.at[slot], r, sem.at[slot]).wait()
        xm_s[...] = _load_row_tiles(xg_s.at[slot], 0, bm, d).astype(MXU_DTYPE)
        acc_s[...] = jnp.zeros_like(acc_s)

    @pl.when(i < n_used)
    def _():
        start_rows(tok_next_ref, 1 - slot, f)
        xm = xm_s[...]
        hidden = jax.nn.silu(_mm(xm, wg_ref[...])) * _mm(xm, wu_ref[...])
        acc_s[...] += _mm(hidden, wd_ref[...])

    last = f == nf - 1

    @pl.when(jnp.logical_and(last, i < n_used))
    def _():
        _store_row_tiles(o_ref, acc_s[...])

    @pl.when(jnp.logical_and(last, i >= n_used))
    def _():
        o_ref[...] = jnp.zeros_like(o_ref)


def _moe_experts(x_rows, block_expert, n_used, buf_tok, wg, wu, wd, *, bm, tf):
    _, per_row, _ = x_rows.shape
    E, D, F = wg.shape
    nb = buf_tok.shape[0]
    nf = F // tf
    per_step = pl.cdiv(bm, nf)

    def w_idx(i, f, be, nu):
        return jnp.where(i < nu[0], f, nf - 1)

    grid_spec = pltpu.PrefetchScalarGridSpec(
        num_scalar_prefetch=2,
        grid=(nb, nf),
        in_specs=[
            pl.BlockSpec((1, 1, bm), lambda i, f, be, nu: (i, 0, 0), memory_space=pltpu.SMEM),
            pl.BlockSpec((1, 1, bm), lambda i, f, be, nu: (jnp.minimum(i + 1, nb - 1), 0, 0),
                         memory_space=pltpu.SMEM),
            pl.BlockSpec(memory_space=pl.ANY),
            pl.BlockSpec((None, D, tf), lambda i, f, be, nu: (be[i], 0, w_idx(i, f, be, nu))),
            pl.BlockSpec((None, D, tf), lambda i, f, be, nu: (be[i], 0, w_idx(i, f, be, nu))),
            pl.BlockSpec((None, tf, D), lambda i, f, be, nu: (be[i], w_idx(i, f, be, nu), 0)),
        ],
        out_specs=pl.BlockSpec((bm * per_row, V7X_LANES), lambda i, f, be, nu: (i, 0)),
        scratch_shapes=[
            pltpu.VMEM((2, nf * per_step * per_row, V7X_LANES), F32),
            pltpu.SemaphoreType.DMA((2,)),
            pltpu.VMEM((bm, D), MXU_DTYPE),
            pltpu.VMEM((bm, D), F32),
        ],
    )
    return pl.pallas_call(
        functools.partial(_moe_kernel, bm=bm, per_step=per_step),
        grid_spec=grid_spec,
        out_shape=jax.ShapeDtypeStruct((nb * bm * per_row, V7X_LANES), F32),
        compiler_params=_cparams("arbitrary", "arbitrary"),
        name="moe_experts",
    )(block_expert, n_used, buf_tok, buf_tok, x_rows, wg, wu, wd)


def _combine_kernel(pos_ref, pos_next_ref, ys_hbm, x_ref, route_ref, g_ref, b_ref, o_ref, yg_s, sem,
                    *, tm, alpha):
    i = pl.program_id(0)
    n = pl.num_programs(0)
    slot = lax.rem(i, 2)
    rows = TOP_K * tm

    def start_gather(p_ref, sl):
        def body(r, c):
            _row_copy(ys_hbm, p_ref[0, 0, r], yg_s.at[sl], r, sem.at[sl]).start()
            return c
        lax.fori_loop(0, rows, body, 0, unroll=8)

    @pl.when(i == 0)
    def _():
        start_gather(pos_ref, 0)

    @pl.when(i + 1 < n)
    def _():
        start_gather(pos_next_ref, 1 - slot)

    for r in range(rows):
        _row_copy(ys_hbm, 0, yg_s.at[slot], r, sem.at[slot]).wait()
    route = route_ref[...]
    x = x_ref[...]
    y = alpha * x
    for k in range(TOP_K):
        y = y + route[:, TOP_K + k:TOP_K + k + 1] * _load_row_tiles(yg_s.at[slot], k * tm, tm, x.shape[1])
    o_ref[...] = _layer_norm(y, g_ref[...], b_ref[...])


def _moe_combine(ys_rows, pos, x2d, route, ln_g, ln_b, *, alpha, tm):
    T, D = x2d.shape
    n = T // tm
    per_row = ys_rows.shape[1]
    return pl.pallas_call(
        functools.partial(_combine_kernel, tm=tm, alpha=alpha),
        grid=(n,),
        in_specs=[
            pl.BlockSpec((1, 1, TOP_K * tm), lambda i: (i, 0, 0), memory_space=pltpu.SMEM),
            pl.BlockSpec((1, 1, TOP_K * tm), lambda i: (jnp.minimum(i + 1, n - 1), 0, 0),
                         memory_space=pltpu.SMEM),
            pl.BlockSpec(memory_space=pl.ANY),
            pl.BlockSpec((tm, D), lambda i: (i, 0)),
            pl.BlockSpec((tm, V7X_LANES), lambda i: (i, 0)),
            pl.BlockSpec((1, D), lambda i: (0, 0)),
            pl.BlockSpec((1, D), lambda i: (0, 0)),
        ],
        out_specs=pl.BlockSpec((tm, D), lambda i: (i, 0)),
        out_shape=jax.ShapeDtypeStruct((T, D), F32),
        scratch_shapes=[pltpu.VMEM((2, TOP_K * tm * per_row, V7X_LANES), F32), pltpu.SemaphoreType.DMA((2,))],
        compiler_params=_cparams("arbitrary"),
        name="moe_combine",
    )(pos, pos, ys_rows, x2d, route, ln_g, ln_b)


def _moe_layer(x2d, x_rows, route, wg, wu, wd, ln_g, ln_b, *, alpha, bm, tf, tm):
    T, D = x2d.shape
    E = wg.shape[0]
    tk = T * TOP_K
    per_row = D // V7X_LANES
    nb = tk // bm + E + 1
    expert = route[:, :TOP_K].astype(jnp.int32).reshape(tk)
    onehot = (expert[:, None] == jnp.arange(E, dtype=jnp.int32)[None, :]).astype(jnp.int32)
    csum = jnp.cumsum(onehot, axis=0)
    counts = csum[-1]
    rank = jnp.sum((csum - onehot) * onehot, axis=1)
    padded = (counts + bm - 1) // bm * bm
    pad_end = jnp.cumsum(padded)
    pad_start = pad_end - padded
    dest = jnp.sum(pad_start[None, :] * onehot, axis=1) + rank
    token = jnp.arange(tk, dtype=jnp.int32) // TOP_K
    buf_tok = jnp.zeros((nb * bm,), jnp.int32).at[dest].set(token)
    block_start = jnp.arange(nb, dtype=jnp.int32) * bm
    block_expert = jnp.minimum(jnp.searchsorted(pad_end, block_start, side="right"), E - 1).astype(jnp.int32)
    n_used = (pad_end[-1] // bm).astype(jnp.int32).reshape(1)
    ys = _moe_experts(x_rows.reshape(T, per_row, V7X_LANES), block_expert, n_used, buf_tok.reshape(nb, 1, bm),
                      wg, wu, wd, bm=bm, tf=tf)
    pos = dest.reshape(T // tm, tm, TOP_K).transpose(0, 2, 1).reshape(T // tm, 1, TOP_K * tm)
    return _moe_combine(ys.reshape(nb * bm, per_row, V7X_LANES), pos, x2d, route, ln_g, ln_b, alpha=alpha, tm=tm)


def _block_diag(w):
    g, n, _ = w.shape
    per = V7X_MXU_DIM // n
    w = w.reshape(g // per, per, n, n)
    eye = jnp.eye(per, dtype=w.dtype)
    return jnp.einsum("apij,pq->apiqj", w, eye).reshape(g // per, V7X_MXU_DIM, V7X_MXU_DIM)


def _pick(n, pref):
    t = min(n, pref)
    while n % t:
        t //= 2
    return t


def kernel(x, w_in, conv_w, conv_b, rg_wr, rg_br, rg_wi, rg_bi, rg_lambda, gla_wa, gla_ba, gla_norm_w, hg_lb_logits, hg_norm_w, w_branch, w_out, ln1_g, ln1_b, ln2_g, ln2_b, ffn_wg, ffn_wu, ffn_wd, router_w, moe_wg, moe_wu, moe_wd):
    B, S, D = x.shape
    depth = w_in.shape[0]
    T = B * S
    alpha = (2 * depth) ** 0.25
    rg_w = conv_w.shape[2]
    g_hk, g_hv = GLA_HEADS * GLA_DK, GLA_HEADS * GLA_DV
    rank = gla_wa.shape[1]
    h_hk, h_hv = HG_HEADS * HG_DK, HG_HEADS * HG_DV
    n_br = w_branch.shape[1]
    sizes = (rg_w, g_hk, g_hk, g_hv, rank, g_hv, h_hk, h_hk, h_hv, h_hv, n_br * D)
    offs = [0]
    for sz in sizes:
        offs.append(offs[-1] + sz)
    assert offs[-1] == w_in.shape[2]

    ts_rg = _pick(S, 512)
    ts_cla = _pick(S, 256)
    tm = _pick(T, 512)
    row2 = lambda v: v.reshape(1, -1)
    cast = lambda w: w.astype(MXU_DTYPE)

    x2d = x.reshape(T, D)
    for l in range(depth):
        wl = w_in[l]
        w_rg = cast(wl[:, offs[0]:offs[1]])
        a_cols = jnp.pad(wl[:, offs[4]:offs[5]], ((0, 0), (0, V7X_LANES - rank)))
        w_gla = cast(jnp.concatenate([wl[:, offs[1]:offs[4]], wl[:, offs[5]:offs[6]], a_cols], axis=1))
        w_hg = cast(wl[:, offs[6]:offs[10]])
        w_gate = cast(wl[:, offs[10]:offs[11]].reshape(D, n_br, D).transpose(1, 0, 2))
        wa = cast(jnp.pad(gla_wa[l], ((0, V7X_LANES - rank), (0, 0))))
        x3d = x2d.reshape(B, S, D)

        y_rg = _rg_branch(x3d, w_rg, conv_w[l], row2(conv_b[l]), cast(_block_diag(rg_wr[l])), row2(rg_br[l]),
                          cast(_block_diag(rg_wi[l])), row2(rg_bi[l]), row2(rg_lambda[l]), ts=ts_rg)
        y_gla = _gla_branch(x3d, w_gla, wa, row2(gla_ba[l]), row2(gla_norm_w[l]), ts=ts_cla)
        y_hg = _hg_branch(x3d, w_hg, hg_lb_logits, row2(hg_norm_w[l]), ts=ts_cla, layer=l)

        is_moe = l % 2 == 1
        j = l // 2
        merged = _merge(x2d, y_rg.reshape(T, -1), y_gla.reshape(T, -1), y_hg.reshape(T, -1), w_gate,
                        cast(w_branch[l]), cast(w_out[l]), row2(ln1_g[l]), row2(ln1_b[l]),
                        router_w[j] if is_moe else None, alpha=alpha, tm=tm)
        if is_moe:
            x1, route, x1_rows = merged
            x2d = _moe_layer(x1, x1_rows, route, cast(moe_wg[j]), cast(moe_wu[j]), cast(moe_wd[j]),
                             row2(ln2_g[l]), row2(ln2_b[l]), alpha=alpha,
                             bm=_pick(T * TOP_K, 512), tf=_pick(moe_wg.shape[3], 512), tm=_pick(T, 256))
        else:
            x2d = _ffn(merged, cast(ffn_wg[j]), cast(ffn_wu[j]), cast(ffn_wd[j]), row2(ln2_g[l]), row2(ln2_b[l]),
                       alpha=alpha, tm=_pick(T, 1024), tf=_pick(ffn_wg.shape[2], 256))
    return x2d.reshape(B, S, D)
```

```python
import functools

import jax
import jax.numpy as jnp
from jax import lax
from jax.experimental import pallas as pl
from jax.experimental.pallas import tpu as pltpu

F32 = jnp.float32
MXU_DTYPE = jnp.bfloat16

RG_C = 8.0
GLA_HEADS, GLA_DK, GLA_DV = 4, 64, 128
GLA_GATE_NORMALIZER = 16.0
HG_HEADS, HG_DK, HG_DV = 4, 128, 128
HG_F_MIN = 1e-30
TOP_K = 2
LN_EPS = 1e-5
RMS_EPS = 1e-6

CHUNK = 64
SUB = 32
PAD = SUB
SAFE_EXP = 60.0

V7X_LANES = 128
V7X_MXU_DIM = 256
V7X_VMEM_BYTES = 64 * 1024 * 1024
VMEM_LIMIT = V7X_VMEM_BYTES * 7 // 8


def _cparams(*sem):
    return pltpu.CompilerParams(dimension_semantics=sem, vmem_limit_bytes=VMEM_LIMIT)


def _mm(a, b):
    return jnp.dot(a.astype(MXU_DTYPE), b.astype(MXU_DTYPE), preferred_element_type=F32)


def _mm_nt(a, b):
    return lax.dot_general(a.astype(MXU_DTYPE), b.astype(MXU_DTYPE), (((1,), (1,)), ((), ())),
                           preferred_element_type=F32)


def _mm_tn(a, b):
    return lax.dot_general(a.astype(MXU_DTYPE), b.astype(MXU_DTYPE), (((0,), (0,)), ((), ())),
                           preferred_element_type=F32)


def _softplus(z):
    return jnp.maximum(z, 0.0) + jnp.log1p(jnp.exp(-jnp.abs(z)))


def _log_sigmoid(z):
    return jnp.minimum(z, 0.0) - jnp.log1p(jnp.exp(-jnp.abs(z)))


def _store_row_tiles(ref, val):
    n, d = val.shape
    per_row = d // V7X_LANES
    for j in range(per_row):
        ref[pl.ds(j, n, stride=per_row), :] = val[:, j * V7X_LANES:(j + 1) * V7X_LANES]


def _load_row_tiles(ref, first_row, n, d):
    per_row = d // V7X_LANES
    return jnp.concatenate(
        [ref[pl.ds(first_row * per_row + j, n, stride=per_row), :] for j in range(per_row)], axis=1)


def _layer_norm(v, g, b):
    mu = jnp.mean(v, axis=-1, keepdims=True)
    d = v - mu
    var = jnp.mean(d * d, axis=-1, keepdims=True)
    return d * lax.rsqrt(var + LN_EPS) * g + b


def _rg_kernel(x_ref, w_ref, cw_ref, cb_ref, wr_ref, br_ref, wi_ref, bi_ref, lam_ref, y_ref,
               tail_ref, h_ref, *, ts, width, taps):
    @pl.when(pl.program_id(1) == 0)
    def _():
        tail_ref[...] = jnp.zeros_like(tail_ref)
        h_ref[...] = jnp.zeros_like(h_ref)

    u = _mm(x_ref[...], w_ref[...])
    ext = jnp.concatenate([tail_ref[...], u], axis=0)
    c = cb_ref[...]
    for j in range(taps):
        off = 8 - (taps - 1) + j
        c = c + ext[off:off + ts] * cw_ref[j:j + 1, :]
    tail_ref[...] = u[ts - 8:]

    cm = c.astype(MXU_DTYPE)
    nblk = width // V7X_MXU_DIM
    zr = jnp.concatenate([_mm(cm[:, n * V7X_MXU_DIM:(n + 1) * V7X_MXU_DIM], wr_ref[n]) for n in range(nblk)], axis=1)
    zi = jnp.concatenate([_mm(cm[:, n * V7X_MXU_DIM:(n + 1) * V7X_MXU_DIM], wi_ref[n]) for n in range(nblk)], axis=1)
    r = jax.nn.sigmoid(zr + br_ref[...])
    i = jax.nn.sigmoid(zi + bi_ref[...])
    log_a = (-RG_C) * r * _softplus(-lam_ref[...])
    a = jnp.exp(log_a)
    b = jnp.sqrt(jnp.maximum(-jnp.tanh(log_a) * (a * a + 1.0), 0.0)) * (i * c)

    row = lax.broadcasted_iota(jnp.int32, (ts, width), 0)
    shift = 1
    while shift < ts:
        valid = row >= shift
        b = jnp.where(valid, a * pltpu.roll(b, shift, 0) + b, b)
        a = jnp.where(valid, a * pltpu.roll(a, shift, 0), a)
        shift *= 2
    h = b + a * h_ref[...]
    h_ref[...] = h[ts - 1:ts, :]
    y_ref[...] = h.astype(y_ref.dtype)


def _rg_branch(x, w, conv_w, conv_b, wr_bd, br, wi_bd, bi, lam, *, ts):
    B, S, D = x.shape
    width = w.shape[1]
    taps = conv_w.shape[0]
    nblk = width // V7X_MXU_DIM
    full = lambda shape: pl.BlockSpec(shape, lambda b, s: (0,) * len(shape))
    return pl.pallas_call(
        functools.partial(_rg_kernel, ts=ts, width=width, taps=taps),
        grid=(B, S // ts),
        in_specs=[
            pl.BlockSpec((None, ts, D), lambda b, s: (b, s, 0)),
            full((D, width)), full((taps, width)), full((1, width)),
            full((nblk, V7X_MXU_DIM, V7X_MXU_DIM)), full((1, width)),
            full((nblk, V7X_MXU_DIM, V7X_MXU_DIM)), full((1, width)), full((1, width)),
        ],
        out_specs=pl.BlockSpec((None, ts, width), lambda b, s: (b, s, 0)),
        out_shape=jax.ShapeDtypeStruct((B, S, width), MXU_DTYPE),
        scratch_shapes=[pltpu.VMEM((8, width), F32), pltpu.VMEM((1, width), F32)],
        compiler_params=_cparams("parallel", "arbitrary"),
        name="rg_lru",
    )(x, w, conv_w, conv_b, wr_bd, br, wi_bd, bi, lam)


def _cla_scratch(ts, heads, dk, dv):
    hk, hv = heads * dk, heads * dv
    return [
        pltpu.VMEM((ts, hk), F32),
        pltpu.VMEM((PAD + ts, hk), F32),
        pltpu.VMEM((PAD + ts, hk), F32),
        pltpu.VMEM((PAD + ts, hv), F32),
        pltpu.VMEM((ts, hk), MXU_DTYPE),
        pltpu.VMEM((ts, hk), MXU_DTYPE),
        pltpu.VMEM((ts, hv), F32),
        pltpu.VMEM((heads, dv, dk), F32),
    ]


def _cla_core(q, k, lg, v, scratch, *, ts, heads, dk, dv):
    q_s, k_s, g_s, v_s, qt_s, qg_s, o_s, st_ref = scratch
    hk = heads * dk
    nsub = CHUNK // SUB

    @pl.when(pl.program_id(1) == 0)
    def _():
        st_ref[...] = jnp.zeros_like(st_ref)
        k_s[0:PAD, :] = jnp.zeros((PAD, hk), F32)
        g_s[0:PAD, :] = jnp.zeros((PAD, hk), F32)
        v_s[0:PAD, :] = jnp.zeros((PAD, heads * dv), F32)

    row = lax.broadcasted_iota(jnp.int32, (ts, hk), 0)
    in_sub = row % SUB
    in_chunk = row % CHUNK
    p = lg
    g = lg
    shift = 1
    while shift < CHUNK:
        if shift < SUB:
            p = p + jnp.where(in_sub >= shift, pltpu.roll(p, shift, 0), 0.0)
        g = g + jnp.where(in_chunk >= shift, pltpu.roll(g, shift, 0), 0.0)
        shift *= 2
    unsafe = jnp.max(-p) > SAFE_EXP

    q_s[...] = q
    k_s[PAD:PAD + ts, :] = k
    g_s[PAD:PAD + ts, :] = g
    v_s[PAD:PAD + ts, :] = v
    qt_s[...] = (q * jnp.exp(p)).astype(MXU_DTYPE)
    qg_s[...] = (q * jnp.exp(g)).astype(MXU_DTYPE)

    ci = lax.broadcasted_iota(jnp.int32, (CHUNK, CHUNK), 0)
    si = lax.broadcasted_iota(jnp.int32, (CHUNK, CHUNK), 1)
    col_limit = jnp.where(unsafe, (ci // SUB) * SUB - 1, ci)
    keep = si <= col_limit

    def chunk_body(n, carry):
        r0 = pl.multiple_of(n * CHUNK, CHUNK)
        gc = g_s[pl.ds(PAD + r0, CHUNK), :]
        kc = k_s[pl.ds(PAD + r0, CHUNK), :]
        vc = v_s[pl.ds(PAD + r0, CHUNK), :].astype(MXU_DTYPE)
        qtc = qt_s[pl.ds(r0, CHUNK), :]
        qgc = qg_s[pl.ds(r0, CHUNK), :]
        g_last = gc[CHUNK - 1:CHUNK, :]
        k_dec = (kc * jnp.exp(g_last - gc)).astype(MXU_DTYPE)
        dec = jnp.exp(g_last)
        k_sub = []
        for i in range(nsub):
            ref = gc[SUB * i - 1:SUB * i, :] if i else jnp.zeros((1, hk), F32)
            k_sub.append((kc * jnp.exp(jnp.minimum(ref - gc, SAFE_EXP))).astype(MXU_DTYPE))
        for h in range(heads):
            ks = slice(h * dk, (h + 1) * dk)
            vs = slice(h * dv, (h + 1) * dv)
            st = st_ref[h]
            o = _mm_nt(qgc[:, ks], st)
            a = jnp.concatenate(
                [_mm_nt(qtc[SUB * i:SUB * (i + 1), ks], k_sub[i][:, ks]) for i in range(nsub)], axis=0)
            a = jnp.where(keep, a, 0.0)
            o = o + _mm(a, vc[:, vs])
            o_s[pl.ds(r0, CHUNK), vs] = o
            st_ref[h] = st * dec[:, ks] + _mm_tn(vc[:, vs], k_dec[:, ks])
        return carry

    lax.fori_loop(0, ts // CHUNK, chunk_body, 0, unroll=True)

    @pl.when(unsafe)
    def _():
        g_t = g_s[PAD:PAD + ts, :]
        q_t = q_s[...]
        for d in range(SUB):
            g_sh = g_s[PAD - d:PAD - d + ts, :]
            k_sh = k_s[PAD - d:PAD - d + ts, :]
            w = jnp.where(in_sub >= d, q_t * k_sh * jnp.exp(jnp.minimum(g_t - g_sh, 0.0)), 0.0)
            for h in range(heads):
                coef = jnp.sum(w[:, h * dk:(h + 1) * dk], axis=1, keepdims=True)
                v_sh = v_s[PAD - d:PAD - d + ts, h * dv:(h + 1) * dv]
                o_s[:, h * dv:(h + 1) * dv] += coef * v_sh

    return o_s[...]


def _gla_kernel(x_ref, w_ref, wa_ref, ba_ref, nw_ref, y_ref, *scratch, ts):
    hk, hv = GLA_HEADS * GLA_DK, GLA_HEADS * GLA_DV
    proj = _mm(x_ref[...], w_ref[...])
    q = proj[:, :hk] * (GLA_DK ** -0.5)
    k = proj[:, hk:2 * hk]
    v = proj[:, 2 * hk:2 * hk + hv]
    gate = proj[:, 2 * hk + hv:2 * hk + 2 * hv]
    a_low = proj[:, 2 * hk + 2 * hv:]
    lg = _log_sigmoid(_mm(a_low, wa_ref[...]) + ba_ref[...]) * (1.0 / GLA_GATE_NORMALIZER)
    o = _cla_core(q, k, lg, v, scratch, ts=ts, heads=GLA_HEADS, dk=GLA_DK, dv=GLA_DV)
    outs = []
    for h in range(GLA_HEADS):
        oh = o[:, h * GLA_DV:(h + 1) * GLA_DV]
        ms = jnp.mean(oh * oh, axis=-1, keepdims=True)
        outs.append(oh * lax.rsqrt(ms + RMS_EPS) * nw_ref[...])
    y_ref[...] = (jnp.concatenate(outs, axis=1) * jax.nn.silu(gate)).astype(y_ref.dtype)


def _hg_kernel(x_ref, w_ref, lbl_ref, nw_ref, y_ref, *scratch, ts, layer):
    hk, hv = HG_HEADS * HG_DK, HG_HEADS * HG_DV
    proj = _mm(x_ref[...], w_ref[...])
    lbl = lbl_ref[...]
    e = jnp.exp(lbl - jnp.max(lbl, axis=0, keepdims=True))
    lb_w = e / jnp.sum(e, axis=0, keepdims=True)
    lb = jnp.zeros((1, hk), F32)
    for l in range(1, layer + 1):
        lb = lb + lb_w[l:l + 1, :]
    z = proj[:, hk:2 * hk]
    forget = lb + (1.0 - lb) * jax.nn.sigmoid(z)
    lg = jnp.log(jnp.maximum(forget, HG_F_MIN))
    k = (1.0 - lb) * jax.nn.sigmoid(-z)
    q = jax.nn.silu(proj[:, :hk]) * (HG_DK ** -0.5)
    v = proj[:, 2 * hk:2 * hk + hv]
    gate = proj[:, 2 * hk + hv:]
    o = _cla_core(q, k, lg, v, scratch, ts=ts, heads=HG_HEADS, dk=HG_DK, dv=HG_DV)
    ms = jnp.mean(o * o, axis=-1, keepdims=True)
    y_ref[...] = (o * lax.rsqrt(ms + RMS_EPS) * nw_ref[...] * jax.nn.sigmoid(gate)).astype(y_ref.dtype)


def _gla_branch(x, w, wa, ba, nw, *, ts):
    B, S, D = x.shape
    hv = GLA_HEADS * GLA_DV
    full = lambda shape: pl.BlockSpec(shape, lambda b, s: (0,) * len(shape))
    return pl.pallas_call(
        functools.partial(_gla_kernel, ts=ts),
        grid=(B, S // ts),
        in_specs=[pl.BlockSpec((None, ts, D), lambda b, s: (b, s, 0)),
                  full(w.shape), full(wa.shape), full(ba.shape), full(nw.shape)],
        out_specs=pl.BlockSpec((None, ts, hv), lambda b, s: (b, s, 0)),
        out_shape=jax.ShapeDtypeStruct((B, S, hv), MXU_DTYPE),
        scratch_shapes=_cla_scratch(ts, GLA_HEADS, GLA_DK, GLA_DV),
        compiler_params=_cparams("parallel", "arbitrary"),
        name="gla",
    )(x, w, wa, ba, nw)


def _hg_branch(x, w, lb_logits, nw, *, ts, layer):
    B, S, D = x.shape
    hv = HG_HEADS * HG_DV
    full = lambda shape: pl.BlockSpec(shape, lambda b, s: (0,) * len(shape))
    return pl.pallas_call(
        functools.partial(_hg_kernel, ts=ts, layer=layer),
        grid=(B, S // ts),
        in_specs=[pl.BlockSpec((None, ts, D), lambda b, s: (b, s, 0)),
                  full(w.shape), full(lb_logits.shape), full(nw.shape)],
        out_specs=pl.BlockSpec((None, ts, hv), lambda b, s: (b, s, 0)),
        out_shape=jax.ShapeDtypeStruct((B, S, hv), MXU_DTYPE),
        scratch_shapes=_cla_scratch(ts, HG_HEADS, HG_DK, HG_DV),
        compiler_params=_cparams("parallel", "arbitrary"),
        name="hgrn2",
    )(x, w, lb_logits, nw)


def _merge_kernel(x_ref, yr_ref, yg_ref, yh_ref, wgate_ref, wbr_ref, wo_ref, g_ref, b_ref, *rest,
                  alpha, n_experts):
    if n_experts:
        rw_ref, out_ref, route_ref, out_rows_ref = rest
    else:
        (out_ref,) = rest
    x = x_ref[...]
    xm = x.astype(MXU_DTYPE)
    mixed = None
    for n, y_ref in enumerate((yr_ref, yg_ref, yh_ref)):
        term = jax.nn.sigmoid(_mm(xm, wgate_ref[n])) * _mm(y_ref[...], wbr_ref[n])
        mixed = term if mixed is None else mixed + term
    out = _layer_norm(alpha * x + _mm(mixed, wo_ref[...]), g_ref[...], b_ref[...])
    out_ref[...] = out
    if n_experts:
        _store_row_tiles(out_rows_ref, out)
        x_hi = out.astype(MXU_DTYPE)
        x_lo = out - x_hi.astype(F32)
        rw = rw_ref[...]
        p_hi = _mm(x_hi, rw)
        logits = p_hi[:, :V7X_LANES] + p_hi[:, V7X_LANES:] + _mm(x_lo, rw[:, :V7X_LANES])
        lane = lax.broadcasted_iota(jnp.int32, logits.shape, 1)
        neg = jnp.float32(-jnp.inf)
        l1 = jnp.where(lane < n_experts, logits, neg)
        m1 = jnp.max(l1, axis=1, keepdims=True)
        i1 = jnp.min(jnp.where(l1 == m1, lane, V7X_LANES), axis=1, keepdims=True)
        l2 = jnp.where(lane == i1, neg, l1)
        m2 = jnp.max(l2, axis=1, keepdims=True)
        i2 = jnp.min(jnp.where(l2 == m2, lane, V7X_LANES), axis=1, keepdims=True)
        e2 = jnp.exp(m2 - m1)
        w1 = 1.0 / (1.0 + e2)
        w2 = e2 / (1.0 + e2)
        route = jnp.where(lane == 0, i1.astype(F32),
                          jnp.where(lane == 1, i2.astype(F32),
                                    jnp.where(lane == 2, w1, jnp.where(lane == 3, w2, 0.0))))
        route_ref[...] = route


def _merge(x2d, y_rg, y_gla, y_hg, w_gate, w_branch, w_out, ln_g, ln_b, router_w, *, alpha, tm):
    T, D = x2d.shape
    bw = y_rg.shape[1]
    n_experts = 0 if router_w is None else router_w.shape[1]
    full = lambda shape: pl.BlockSpec(shape, lambda i: (0,) * len(shape))
    row = lambda width: pl.BlockSpec((tm, width), lambda i: (i, 0))
    in_specs = [row(D), row(bw), row(bw), row(bw), full(w_gate.shape), full(w_branch.shape),
                full(w_out.shape), full((1, D)), full((1, D))]
    args = [x2d, y_rg, y_gla, y_hg, w_gate, w_branch, w_out, ln_g, ln_b]
    out_specs = row(D)
    out_shape = jax.ShapeDtypeStruct((T, D), F32)
    if n_experts:
        rw = jnp.pad(router_w, ((0, 0), (0, V7X_LANES - n_experts)))
        rw_hi = rw.astype(MXU_DTYPE)
        rw = jnp.concatenate([rw_hi, (rw - rw_hi.astype(F32)).astype(MXU_DTYPE)], axis=1)
        in_specs.append(full(rw.shape))
        args.append(rw)
        per_row = D // V7X_LANES
        out_specs = (out_specs, row(V7X_LANES), pl.BlockSpec((tm * per_row, V7X_LANES), lambda i: (i, 0)))
        out_shape = (out_shape, jax.ShapeDtypeStruct((T, V7X_LANES), F32),
                     jax.ShapeDtypeStruct((T * per_row, V7X_LANES), F32))
    return pl.pallas_call(
        functools.partial(_merge_kernel, alpha=alpha, n_experts=n_experts),
        grid=(T // tm,),
        in_specs=in_specs,
        out_specs=out_specs,
        out_shape=out_shape,
        compiler_params=_cparams("parallel"),
        name="merge_router" if n_experts else "merge",
    )(*args)


def _ffn_kernel(x_ref, wg_ref, wu_ref, wd_ref, g_ref, b_ref, o_ref, xm_s, acc_s, *, alpha):
    f = pl.program_id(1)

    @pl.when(f == 0)
    def _():
        xm_s[...] = x_ref[...].astype(MXU_DTYPE)
        acc_s[...] = jnp.zeros_like(acc_s)

    xm = xm_s[...]
    hidden = jax.nn.silu(_mm(xm, wg_ref[...])) * _mm(xm, wu_ref[...])
    acc_s[...] += _mm(hidden, wd_ref[...])

    @pl.when(f == pl.num_programs(1) - 1)
    def _():
        o_ref[...] = _layer_norm(alpha * x_ref[...] + acc_s[...], g_ref[...], b_ref[...])


def _ffn(x2d, wg, wu, wd, ln_g, ln_b, *, alpha, tm, tf):
    T, D = x2d.shape
    F = wg.shape[1]
    return pl.pallas_call(
        functools.partial(_ffn_kernel, alpha=alpha),
        grid=(T // tm, F // tf),
        in_specs=[
            pl.BlockSpec((tm, D), lambda i, f: (i, 0)),
            pl.BlockSpec((D, tf), lambda i, f: (0, f)),
            pl.BlockSpec((D, tf), lambda i, f: (0, f)),
            pl.BlockSpec((tf, D), lambda i, f: (f, 0)),
            pl.BlockSpec((1, D), lambda i, f: (0, 0)),
            pl.BlockSpec((1, D), lambda i, f: (0, 0)),
        ],
        out_specs=pl.BlockSpec((tm, D), lambda i, f: (i, 0)),
        out_shape=jax.ShapeDtypeStruct((T, D), F32),
        scratch_shapes=[pltpu.VMEM((tm, D), MXU_DTYPE), pltpu.VMEM((tm, D), F32)],
        compiler_params=_cparams("parallel", "arbitrary"),
        name="ffn_swiglu",
    )(x2d, wg, wu, wd, ln_g, ln_b)


def _row_copy(src_hbm, src_row, dst, dst_row, sem):
    per_row = src_hbm.shape[1]
    return pltpu.make_async_copy(src_hbm.at[src_row], dst.at[pl.ds(dst_row * per_row, per_row), :], sem)


def _moe_kernel(be_ref, nb_ref, tok_ref, tok_next_ref, x_hbm, wg_ref, wu_ref, wd_ref, o_ref,
                xg_s, sem, xm_s, acc_s, *, bm, per_step):
    i = pl.program_id(0)
    f = pl.program_id(1)
    nf = pl.num_programs(1)
    n_used = nb_ref[0]
    slot = lax.rem(i, 2)
    d = xm_s.shape[1]

    def start_rows(t_ref, sl, step):
        for j in range(per_step):
            r = step * per_step + j
            _row_copy(x_hbm, t_ref[0, 0, jnp.minimum(r, bm - 1)], xg_s.at[sl], r, sem.at[sl]).start()

    @pl.when(jnp.logical_and(f == 0, i == 0))
    def _():
        def body(step, c):
            start_rows(tok_ref, 0, step)
            return c
        lax.fori_loop(0, nf, body, 0)

    @pl.when(jnp.logical_and(f == 0, i <= n_used))
    def _():
        for r in range(xg_s.shape[1] // x_hbm.shape[1]):
            _row_copy(x_hbm, 0, xg_s.at[slot], r, sem.at[slot]).wait()
        xm_s[...] = _load_row_tiles(xg_s.at[slot], 0, bm, d).astype(MXU_DTYPE)
        acc_s[...] = jnp.zeros_like(acc_s)

    @pl.when(i < n_used)
    def _():
        start_rows(tok_next_ref, 1 - slot, f)
        xm = xm_s[...]
        hidden = jax.nn.silu(_mm(xm, wg_ref[...])) * _mm(xm, wu_ref[...])
        acc_s[...] += _mm(hidden, wd_ref[...])

    last = f == nf - 1

    @pl.when(jnp.logical_and(last, i < n_used))
    def _():
        _store_row_tiles(o_ref, acc_s[...])

    @pl.when(jnp.logical_and(last, i >= n_used))
    def _():
        o_ref[...] = jnp.zeros_like(o_ref)


def _moe_experts(x_rows, block_expert, n_used, buf_tok, wg, wu, wd, *, bm, tf):
    _, per_row, _ = x_rows.shape
    E, D, F = wg.shape
    nb = buf_tok.shape[0]
    nf = F // tf
    per_step = pl.cdiv(bm, nf)

    def w_idx(i, f, be, nu):
        return jnp.where(i < nu[0], f, nf - 1)

    grid_spec = pltpu.PrefetchScalarGridSpec(
        num_scalar_prefetch=2,
        grid=(nb, nf),
        in_specs=[
            pl.BlockSpec((1, 1, bm), lambda i, f, be, nu: (i, 0, 0), memory_space=pltpu.SMEM),
            pl.BlockSpec((1, 1, bm), lambda i, f, be, nu: (jnp.minimum(i + 1, nb - 1), 0, 0),
                         memory_space=pltpu.SMEM),
            pl.BlockSpec(memory_space=pl.ANY),
            pl.BlockSpec((None, D, tf), lambda i, f, be, nu: (be[i], 0, w_idx(i, f, be, nu))),
            pl.BlockSpec((None, D, tf), lambda i, f, be, nu: (be[i], 0, w_idx(i, f, be, nu))),
            pl.BlockSpec((None, tf, D), lambda i, f, be, nu: (be[i], w_idx(i, f, be, nu), 0)),
        ],
        out_specs=pl.BlockSpec((bm * per_row, V7X_LANES), lambda i, f, be, nu: (i, 0)),
        scratch_shapes=[
            pltpu.VMEM((2, nf * per_step * per_row, V7X_LANES), F32),
            pltpu.SemaphoreType.DMA((2,)),
            pltpu.VMEM((bm, D), MXU_DTYPE),
            pltpu.VMEM((bm, D), F32),
        ],
    )
    return pl.pallas_call(
        functools.partial(_moe_kernel, bm=bm, per_step=per_step),
        grid_spec=grid_spec,
        out_shape=jax.ShapeDtypeStruct((nb * bm * per_row, V7X_LANES), F32),
        compiler_params=_cparams("arbitrary", "arbitrary"),
        name="moe_experts",
    )(block_expert, n_used, buf_tok, buf_tok, x_rows, wg, wu, wd)


def _combine_kernel(pos_ref, pos_next_ref, ys_hbm, x_ref, route_ref, g_ref, b_ref, o_ref, yg_s, sem,
                    *, tm, alpha):
    i = pl.program_id(0)
    n = pl.num_programs(0)
    slot = lax.rem(i, 2)
    rows = TOP_K * tm

    def start_gather(p_ref, sl):
        def body(r, c):
            _row_copy(ys_hbm, p_ref[0, 0, r], yg_s.at[sl], r, sem.at[sl]).start()
            return c
        lax.fori_loop(0, rows, body, 0, unroll=8)

    @pl.when(i == 0)
    def _():
        start_gather(pos_ref, 0)

    @pl.when(i + 1 < n)
    def _():
        start_gather(pos_next_ref, 1 - slot)

    for r in range(rows):
        _row_copy(ys_hbm, 0, yg_s.at[slot], r, sem.at[slot]).wait()
    route = route_ref[...]
    x = x_ref[...]
    y = alpha * x
    for k in range(TOP_K):
        y = y + route[:, TOP_K + k:TOP_K + k + 1] * _load_row_tiles(yg_s.at[slot], k * tm, tm, x.shape[1])
    o_ref[...] = _layer_norm(y, g_ref[...], b_ref[...])


def _moe_combine(ys_rows, pos, x2d, route, ln_g, ln_b, *, alpha, tm):
    T, D = x2d.shape
    n = T // tm
    per_row = ys_rows.shape[1]
    return pl.pallas_call(
        functools.partial(_combine_kernel, tm=tm, alpha=alpha),
        grid=(n,),
        in_specs=[
            pl.BlockSpec((1, 1, TOP_K * tm), lambda i: (i, 0, 0), memory_space=pltpu.SMEM),
            pl.BlockSpec((1, 1, TOP_K * tm), lambda i: (jnp.minimum(i + 1, n - 1), 0, 0),
                         memory_space=pltpu.SMEM),
            pl.BlockSpec(memory_space=pl.ANY),
            pl.BlockSpec((tm, D), lambda i: (i, 0)),
            pl.BlockSpec((tm, V7X_LANES), lambda i: (i, 0)),
            pl.BlockSpec((1, D), lambda i: (0, 0)),
            pl.BlockSpec((1, D), lambda i: (0, 0)),
        ],
        out_specs=pl.BlockSpec((tm, D), lambda i: (i, 0)),
        out_shape=jax.ShapeDtypeStruct((T, D), F32),
        scratch_shapes=[pltpu.VMEM((2, TOP_K * tm * per_row, V7X_LANES), F32), pltpu.SemaphoreType.DMA((2,))],
        compiler_params=_cparams("arbitrary"),
        name="moe_combine",
    )(pos, pos, ys_rows, x2d, route, ln_g, ln_b)


def _moe_layer(x2d, x_rows, route, wg, wu, wd, ln_g, ln_b, *, alpha, bm, tf, tm):
    T, D = x2d.shape
    E = wg.shape[0]
    tk = T * TOP_K
    per_row = D // V7X_LANES
    nb = tk // bm + E + 1
    expert = route[:, :TOP_K].astype(jnp.int32).reshape(tk)
    onehot = (expert[:, None] == jnp.arange(E, dtype=jnp.int32)[None, :]).astype(jnp.int32)
    csum = jnp.cumsum(onehot, axis=0)
    counts = csum[-1]
    rank = jnp.sum((csum - onehot) * onehot, axis=1)
    padded = (counts + bm - 1) // bm * bm
    pad_end = jnp.cumsum(padded)
    pad_start = pad_end - padded
    dest = jnp.sum(pad_start[None, :] * onehot, axis=1) + rank
    token = jnp.arange(tk, dtype=jnp.int32) // TOP_K
    buf_tok = jnp.zeros((nb * bm,), jnp.int32).at[dest].set(token)
    block_start = jnp.arange(nb, dtype=jnp.int32) * bm
    block_expert = jnp.minimum(jnp.searchsorted(pad_end, block_start, side="right"), E - 1).astype(jnp.int32)
    n_used = (pad_end[-1] // bm).astype(jnp.int32).reshape(1)
    ys = _moe_experts(x_rows.reshape(T, per_row, V7X_LANES), block_expert, n_used, buf_tok.reshape(nb, 1, bm),
                      wg, wu, wd, bm=bm, tf=tf)
    pos = dest.reshape(T // tm, tm, TOP_K).transpose(0, 2, 1).reshape(T // tm, 1, TOP_K * tm)
    return _moe_combine(ys.reshape(nb * bm, per_row, V7X_LANES), pos, x2d, route, ln_g, ln_b, alpha=alpha, tm=tm)


def _block_diag(w):
    g, n, _ = w.shape
    per = V7X_MXU_DIM // n
    w = w.reshape(g // per, per, n, n)
    eye = jnp.eye(per, dtype=w.dtype)
    return jnp.einsum("apij,pq->apiqj", w, eye).reshape(g // per, V7X_MXU_DIM, V7X_MXU_DIM)


def _pick(n, pref):
    t = min(n, pref)
    while n % t:
        t //= 2
    return t


def kernel(x, w_in, conv_w, conv_b, rg_wr, rg_br, rg_wi, rg_bi, rg_lambda, gla_wa, gla_ba, gla_norm_w, hg_lb_logits, hg_norm_w, w_branch, w_out, ln1_g, ln1_b, ln2_g, ln2_b, ffn_wg, ffn_wu, ffn_wd, router_w, moe_wg, moe_wu, moe_wd):
    B, S, D = x.shape
    depth = w_in.shape[0]
    T = B * S
    alpha = (2 * depth) ** 0.25
    rg_w = conv_w.shape[2]
    g_hk, g_hv = GLA_HEADS * GLA_DK, GLA_HEADS * GLA_DV
    rank = gla_wa.shape[1]
    h_hk, h_hv = HG_HEADS * HG_DK, HG_HEADS * HG_DV
    n_br = w_branch.shape[1]
    sizes = (rg_w, g_hk, g_hk, g_hv, rank, g_hv, h_hk, h_hk, h_hv, h_hv, n_br * D)
    offs = [0]
    for sz in sizes:
        offs.append(offs[-1] + sz)
    assert offs[-1] == w_in.shape[2]

    ts_rg = _pick(S, 512)
    ts_cla = _pick(S, 256)
    tm = _pick(T, 512)
    row2 = lambda v: v.reshape(1, -1)
    cast = lambda w: w.astype(MXU_DTYPE)

    x2d = x.reshape(T, D)
    for l in range(depth):
        wl = w_in[l]
        w_rg = cast(wl[:, offs[0]:offs[1]])
        a_cols = jnp.pad(wl[:, offs[4]:offs[5]], ((0, 0), (0, V7X_LANES - rank)))
        w_gla = cast(jnp.concatenate([wl[:, offs[1]:offs[4]], wl[:, offs[5]:offs[6]], a_cols], axis=1))
        w_hg = cast(wl[:, offs[6]:offs[10]])
        w_gate = cast(wl[:, offs[10]:offs[11]].reshape(D, n_br, D).transpose(1, 0, 2))
        wa = cast(jnp.pad(gla_wa[l], ((0, V7X_LANES - rank), (0, 0))))
        x3d = x2d.reshape(B, S, D)

        y_rg = _rg_branch(x3d, w_rg, conv_w[l], row2(conv_b[l]), cast(_block_diag(rg_wr[l])), row2(rg_br[l]),
                          cast(_block_diag(rg_wi[l])), row2(rg_bi[l]), row2(rg_lambda[l]), ts=ts_rg)
        y_gla = _gla_branch(x3d, w_gla, wa, row2(gla_ba[l]), row2(gla_norm_w[l]), ts=ts_cla)
        y_hg = _hg_branch(x3d, w_hg, hg_lb_logits, row2(hg_norm_w[l]), ts=ts_cla, layer=l)

        is_moe = l % 2 == 1
        j = l // 2
        merged = _merge(x2d, y_rg.reshape(T, -1), y_gla.reshape(T, -1), y_hg.reshape(T, -1), w_gate,
                        cast(w_branch[l]), cast(w_out[l]), row2(ln1_g[l]), row2(ln1_b[l]),
                        router_w[j] if is_moe else None, alpha=alpha, tm=tm)
        if is_moe:
            x1, route, x1_rows = merged
            x2d = _moe_layer(x1, x1_rows, route, cast(moe_wg[j]), cast(moe_wu[j]), cast(moe_wd[j]),
                             row2(ln2_g[l]), row2(ln2_b[l]), alpha=alpha,
                             bm=_pick(T * TOP_K, 1024), tf=_pick(moe_wg.shape[3], 512), tm=_pick(T, 256))
        else:
            x2d = _ffn(merged, cast(ffn_wg[j]), cast(ffn_wu[j]), cast(ffn_wd[j]), row2(ln2_g[l]), row2(ln2_b[l]),
                       alpha=alpha, tm=_pick(T, 1024), tf=_pick(ffn_wg.shape[2], 256))
    return x2d.reshape(B, S, D)
```

```python
import functools

import jax
import jax.numpy as jnp
from jax import lax
from jax.experimental import pallas as pl
from jax.experimental.pallas import tpu as pltpu

F32 = jnp.float32
MXU_DTYPE = jnp.bfloat16

RG_C = 8.0
GLA_HEADS, GLA_DK, GLA_DV = 4, 64, 128
GLA_GATE_NORMALIZER = 16.0
HG_HEADS, HG_DK, HG_DV = 4, 128, 128
HG_F_MIN = 1e-30
TOP_K = 2
LN_EPS = 1e-5
RMS_EPS = 1e-6

CHUNK = 64
SUB = 32
PAD = SUB
SAFE_EXP = 60.0

V7X_LANES = 128
V7X_MXU_DIM = 256
V7X_VMEM_BYTES = 64 * 1024 * 1024
VMEM_LIMIT = V7X_VMEM_BYTES * 7 // 8


def _cparams(*sem):
    return pltpu.CompilerParams(dimension_semantics=sem, vmem_limit_bytes=VMEM_LIMIT)


def _mm(a, b):
    return jnp.dot(a.astype(MXU_DTYPE), b.astype(MXU_DTYPE), preferred_element_type=F32)


def _mm_nt(a, b):
    return lax.dot_general(a.astype(MXU_DTYPE), b.astype(MXU_DTYPE), (((1,), (1,)), ((), ())),
                           preferred_element_type=F32)


def _mm_tn(a, b):
    return lax.dot_general(a.astype(MXU_DTYPE), b.astype(MXU_DTYPE), (((0,), (0,)), ((), ())),
                           preferred_element_type=F32)


def _softplus(z):
    return jnp.maximum(z, 0.0) + jnp.log1p(jnp.exp(-jnp.abs(z)))


def _log_sigmoid(z):
    return jnp.minimum(z, 0.0) - jnp.log1p(jnp.exp(-jnp.abs(z)))


def _store_row_tiles(ref, val):
    n, d = val.shape
    per_row = d // V7X_LANES
    for j in range(per_row):
        ref[pl.ds(j, n, stride=per_row), :] = val[:, j * V7X_LANES:(j + 1) * V7X_LANES]


def _load_row_tiles(ref, first_row, n, d):
    per_row = d // V7X_LANES
    return jnp.concatenate(
        [ref[pl.ds(first_row * per_row + j, n, stride=per_row), :] for j in range(per_row)], axis=1)


def _layer_norm(v, g, b):
    mu = jnp.mean(v, axis=-1, keepdims=True)
    d = v - mu
    var = jnp.mean(d * d, axis=-1, keepdims=True)
    return d * lax.rsqrt(var + LN_EPS) * g + b


def _rg_kernel(x_ref, w_ref, cw_ref, cb_ref, wr_ref, br_ref, wi_ref, bi_ref, lam_ref, y_ref,
               tail_ref, h_ref, *, ts, width, taps):
    @pl.when(pl.program_id(1) == 0)
    def _():
        tail_ref[...] = jnp.zeros_like(tail_ref)
        h_ref[...] = jnp.zeros_like(h_ref)

    u = _mm(x_ref[...], w_ref[...])
    ext = jnp.concatenate([tail_ref[...], u], axis=0)
    c = cb_ref[...]
    for j in range(taps):
        off = 8 - (taps - 1) + j
        c = c + ext[off:off + ts] * cw_ref[j:j + 1, :]
    tail_ref[...] = u[ts - 8:]

    cm = c.astype(MXU_DTYPE)
    nblk = width // V7X_MXU_DIM
    zr = jnp.concatenate([_mm(cm[:, n * V7X_MXU_DIM:(n + 1) * V7X_MXU_DIM], wr_ref[n]) for n in range(nblk)], axis=1)
    zi = jnp.concatenate([_mm(cm[:, n * V7X_MXU_DIM:(n + 1) * V7X_MXU_DIM], wi_ref[n]) for n in range(nblk)], axis=1)
    r = jax.nn.sigmoid(zr + br_ref[...])
    i = jax.nn.sigmoid(zi + bi_ref[...])
    log_a = (-RG_C) * r * _softplus(-lam_ref[...])
    a = jnp.exp(log_a)
    b = jnp.sqrt(jnp.maximum(-jnp.tanh(log_a) * (a * a + 1.0), 0.0)) * (i * c)

    row = lax.broadcasted_iota(jnp.int32, (ts, width), 0)
    shift = 1
    while shift < ts:
        valid = row >= shift
        b = jnp.where(valid, a * pltpu.roll(b, shift, 0) + b, b)
        a = jnp.where(valid, a * pltpu.roll(a, shift, 0), a)
        shift *= 2
    h = b + a * h_ref[...]
    h_ref[...] = h[ts - 1:ts, :]
    y_ref[...] = h.astype(y_ref.dtype)


def _rg_branch(x, w, conv_w, conv_b, wr_bd, br, wi_bd, bi, lam, *, ts):
    B, S, D = x.shape
    width = w.shape[1]
    taps = conv_w.shape[0]
    nblk = width // V7X_MXU_DIM
    full = lambda shape: pl.BlockSpec(shape, lambda b, s: (0,) * len(shape))
    return pl.pallas_call(
        functools.partial(_rg_kernel, ts=ts, width=width, taps=taps),
        grid=(B, S // ts),
        in_specs=[
            pl.BlockSpec((None, ts, D), lambda b, s: (b, s, 0)),
            full((D, width)), full((taps, width)), full((1, width)),
            full((nblk, V7X_MXU_DIM, V7X_MXU_DIM)), full((1, width)),
            full((nblk, V7X_MXU_DIM, V7X_MXU_DIM)), full((1, width)), full((1, width)),
        ],
        out_specs=pl.BlockSpec((None, ts, width), lambda b, s: (b, s, 0)),
        out_shape=jax.ShapeDtypeStruct((B, S, width), MXU_DTYPE),
        scratch_shapes=[pltpu.VMEM((8, width), F32), pltpu.VMEM((1, width), F32)],
        compiler_params=_cparams("parallel", "arbitrary"),
        name="rg_lru",
    )(x, w, conv_w, conv_b, wr_bd, br, wi_bd, bi, lam)


def _cla_scratch(ts, heads, dk, dv):
    hk, hv = heads * dk, heads * dv
    return [
        pltpu.VMEM((ts, hk), F32),
        pltpu.VMEM((PAD + ts, hk), F32),
        pltpu.VMEM((PAD + ts, hk), F32),
        pltpu.VMEM((PAD + ts, hv), F32),
        pltpu.VMEM((ts, hk), MXU_DTYPE),
        pltpu.VMEM((ts, hk), MXU_DTYPE),
        pltpu.VMEM((ts, hv), F32),
        pltpu.VMEM((heads, dv, dk), F32),
    ]


def _cla_core(q, k, lg, v, scratch, *, ts, heads, dk, dv):
    q_s, k_s, g_s, v_s, qt_s, qg_s, o_s, st_ref = scratch
    hk = heads * dk
    nsub = CHUNK // SUB

    @pl.when(pl.program_id(1) == 0)
    def _():
        st_ref[...] = jnp.zeros_like(st_ref)
        k_s[0:PAD, :] = jnp.zeros((PAD, hk), F32)
        g_s[0:PAD, :] = jnp.zeros((PAD, hk), F32)
        v_s[0:PAD, :] = jnp.zeros((PAD, heads * dv), F32)

    row = lax.broadcasted_iota(jnp.int32, (ts, hk), 0)
    in_sub = row % SUB
    in_chunk = row % CHUNK
    p = lg
    g = lg
    shift = 1
    while shift < CHUNK:
        if shift < SUB:
            p = p + jnp.where(in_sub >= shift, pltpu.roll(p, shift, 0), 0.0)
        g = g + jnp.where(in_chunk >= shift, pltpu.roll(g, shift, 0), 0.0)
        shift *= 2
    unsafe = jnp.max(-p) > SAFE_EXP

    q_s[...] = q
    k_s[PAD:PAD + ts, :] = k
    g_s[PAD:PAD + ts, :] = g
    v_s[PAD:PAD + ts, :] = v
    qt_s[...] = (q * jnp.exp(p)).astype(MXU_DTYPE)
    qg_s[...] = (q * jnp.exp(g)).astype(MXU_DTYPE)

    ci = lax.broadcasted_iota(jnp.int32, (CHUNK, CHUNK), 0)
    si = lax.broadcasted_iota(jnp.int32, (CHUNK, CHUNK), 1)
    col_limit = jnp.where(unsafe, (ci // SUB) * SUB - 1, ci)
    keep = si <= col_limit

    def chunk_body(n, carry):
        r0 = pl.multiple_of(n * CHUNK, CHUNK)
        gc = g_s[pl.ds(PAD + r0, CHUNK), :]
        kc = k_s[pl.ds(PAD + r0, CHUNK), :]
        vc = v_s[pl.ds(PAD + r0, CHUNK), :].astype(MXU_DTYPE)
        qtc = qt_s[pl.ds(r0, CHUNK), :]
        qgc = qg_s[pl.ds(r0, CHUNK), :]
        g_last = gc[CHUNK - 1:CHUNK, :]
        k_dec = (kc * jnp.exp(g_last - gc)).astype(MXU_DTYPE)
        dec = jnp.exp(g_last)
        k_sub = []
        for i in range(nsub):
            ref = gc[SUB * i - 1:SUB * i, :] if i else jnp.zeros((1, hk), F32)
            k_sub.append((kc * jnp.exp(jnp.minimum(ref - gc, SAFE_EXP))).astype(MXU_DTYPE))
        for h in range(heads):
            ks = slice(h * dk, (h + 1) * dk)
            vs = slice(h * dv, (h + 1) * dv)
            st = st_ref[h]
            o = _mm_nt(qgc[:, ks], st)
            a = jnp.concatenate(
                [_mm_nt(qtc[SUB * i:SUB * (i + 1), ks], k_sub[i][:, ks]) for i in range(nsub)], axis=0)
            a = jnp.where(keep, a, 0.0)
            o = o + _mm(a, vc[:, vs])
            o_s[pl.ds(r0, CHUNK), vs] = o
            st_ref[h] = st * dec[:, ks] + _mm_tn(vc[:, vs], k_dec[:, ks])
        return carry

    lax.fori_loop(0, ts // CHUNK, chunk_body, 0, unroll=True)

    @pl.when(unsafe)
    def _():
        g_t = g_s[PAD:PAD + ts, :]
        q_t = q_s[...]
        for d in range(SUB):
            g_sh = g_s[PAD - d:PAD - d + ts, :]
            k_sh = k_s[PAD - d:PAD - d + ts, :]
            w = jnp.where(in_sub >= d, q_t * k_sh * jnp.exp(jnp.minimum(g_t - g_sh, 0.0)), 0.0)
            for h in range(heads):
                coef = jnp.sum(w[:, h * dk:(h + 1) * dk], axis=1, keepdims=True)
                v_sh = v_s[PAD - d:PAD - d + ts, h * dv:(h + 1) * dv]
                o_s[:, h * dv:(h + 1) * dv] += coef * v_sh

    return o_s[...]


def _gla_kernel(x_ref, w_ref, wa_ref, ba_ref, nw_ref, y_ref, *scratch, ts):
    hk, hv = GLA_HEADS * GLA_DK, GLA_HEADS * GLA_DV
    proj = _mm(x_ref[...], w_ref[...])
    q = proj[:, :hk] * (GLA_DK ** -0.5)
    k = proj[:, hk:2 * hk]
    v = proj[:, 2 * hk:2 * hk + hv]
    gate = proj[:, 2 * hk + hv:2 * hk + 2 * hv]
    a_low = proj[:, 2 * hk + 2 * hv:]
    lg = _log_sigmoid(_mm(a_low, wa_ref[...]) + ba_ref[...]) * (1.0 / GLA_GATE_NORMALIZER)
    o = _cla_core(q, k, lg, v, scratch, ts=ts, heads=GLA_HEADS, dk=GLA_DK, dv=GLA_DV)
    outs = []
    for h in range(GLA_HEADS):
        oh = o[:, h * GLA_DV:(h + 1) * GLA_DV]
        ms = jnp.mean(oh * oh, axis=-1, keepdims=True)
        outs.append(oh * lax.rsqrt(ms + RMS_EPS) * nw_ref[...])
    y_ref[...] = (jnp.concatenate(outs, axis=1) * jax.nn.silu(gate)).astype(y_ref.dtype)


def _hg_kernel(x_ref, w_ref, lbl_ref, nw_ref, y_ref, *scratch, ts, layer):
    hk, hv = HG_HEADS * HG_DK, HG_HEADS * HG_DV
    proj = _mm(x_ref[...], w_ref[...])
    lbl = lbl_ref[...]
    e = jnp.exp(lbl - jnp.max(lbl, axis=0, keepdims=True))
    lb_w = e / jnp.sum(e, axis=0, keepdims=True)
    lb = jnp.zeros((1, hk), F32)
    for l in range(1, layer + 1):
        lb = lb + lb_w[l:l + 1, :]
    z = proj[:, hk:2 * hk]
    forget = lb + (1.0 - lb) * jax.nn.sigmoid(z)
    lg = jnp.log(jnp.maximum(forget, HG_F_MIN))
    k = (1.0 - lb) * jax.nn.sigmoid(-z)
    q = jax.nn.silu(proj[:, :hk]) * (HG_DK ** -0.5)
    v = proj[:, 2 * hk:2 * hk + hv]
    gate = proj[:, 2 * hk + hv:]
    o = _cla_core(q, k, lg, v, scratch, ts=ts, heads=HG_HEADS, dk=HG_DK, dv=HG_DV)
    ms = jnp.mean(o * o, axis=-1, keepdims=True)
    y_ref[...] = (o * lax.rsqrt(ms + RMS_EPS) * nw_ref[...] * jax.nn.sigmoid(gate)).astype(y_ref.dtype)


def _gla_branch(x, w, wa, ba, nw, *, ts):
    B, S, D = x.shape
    hv = GLA_HEADS * GLA_DV
    full = lambda shape: pl.BlockSpec(shape, lambda b, s: (0,) * len(shape))
    return pl.pallas_call(
        functools.partial(_gla_kernel, ts=ts),
        grid=(B, S // ts),
        in_specs=[pl.BlockSpec((None, ts, D), lambda b, s: (b, s, 0)),
                  full(w.shape), full(wa.shape), full(ba.shape), full(nw.shape)],
        out_specs=pl.BlockSpec((None, ts, hv), lambda b, s: (b, s, 0)),
        out_shape=jax.ShapeDtypeStruct((B, S, hv), MXU_DTYPE),
        scratch_shapes=_cla_scratch(ts, GLA_HEADS, GLA_DK, GLA_DV),
        compiler_params=_cparams("parallel", "arbitrary"),
        name="gla",
    )(x, w, wa, ba, nw)


def _hg_branch(x, w, lb_logits, nw, *, ts, layer):
    B, S, D = x.shape
    hv = HG_HEADS * HG_DV
    full = lambda shape: pl.BlockSpec(shape, lambda b, s: (0,) * len(shape))
    return pl.pallas_call(
        functools.partial(_hg_kernel, ts=ts, layer=layer),
        grid=(B, S // ts),
        in_specs=[pl.BlockSpec((None, ts, D), lambda b, s: (b, s, 0)),
                  full(w.shape), full(lb_logits.shape), full(nw.shape)],
        out_specs=pl.BlockSpec((None, ts, hv), lambda b, s: (b, s, 0)),
        out_shape=jax.ShapeDtypeStruct((B, S, hv), MXU_DTYPE),
        scratch_shapes=_cla_scratch(ts, HG_HEADS, HG_DK, HG_DV),
        compiler_params=_cparams("parallel", "arbitrary"),
        name="hgrn2",
    )(x, w, lb_logits, nw)


def _merge_kernel(x_ref, yr_ref, yg_ref, yh_ref, wgate_ref, wbr_ref, wo_ref, g_ref, b_ref, *rest,
                  alpha, n_experts):
    if n_experts:
        rw_ref, out_ref, route_ref, out_rows_ref = rest
    else:
        (out_ref,) = rest
    x = x_ref[...]
    xm = x.astype(MXU_DTYPE)
    mixed = None
    for n, y_ref in enumerate((yr_ref, yg_ref, yh_ref)):
        term = jax.nn.sigmoid(_mm(xm, wgate_ref[n])) * _mm(y_ref[...], wbr_ref[n])
        mixed = term if mixed is None else mixed + term
    out = _layer_norm(alpha * x + _mm(mixed, wo_ref[...]), g_ref[...], b_ref[...])
    out_ref[...] = out
    if n_experts:
        _store_row_tiles(out_rows_ref, out)
        x_hi = out.astype(MXU_DTYPE)
        x_lo = out - x_hi.astype(F32)
        rw = rw_ref[...]
        p_hi = _mm(x_hi, rw)
        logits = p_hi[:, :V7X_LANES] + p_hi[:, V7X_LANES:] + _mm(x_lo, rw[:, :V7X_LANES])
        lane = lax.broadcasted_iota(jnp.int32, logits.shape, 1)
        neg = jnp.float32(-jnp.inf)
        l1 = jnp.where(lane < n_experts, logits, neg)
        m1 = jnp.max(l1, axis=1, keepdims=True)
        i1 = jnp.min(jnp.where(l1 == m1, lane, V7X_LANES), axis=1, keepdims=True)
        l2 = jnp.where(lane == i1, neg, l1)
        m2 = jnp.max(l2, axis=1, keepdims=True)
        i2 = jnp.min(jnp.where(l2 == m2, lane, V7X_LANES), axis=1, keepdims=True)
        e2 = jnp.exp(m2 - m1)
        w1 = 1.0 / (1.0 + e2)
        w2 = e2 / (1.0 + e2)
        route = jnp.where(lane == 0, i1.astype(F32),
                          jnp.where(lane == 1, i2.astype(F32),
                                    jnp.where(lane == 2, w1, jnp.where(lane == 3, w2, 0.0))))
        route_ref[...] = route


def _merge(x2d, y_rg, y_gla, y_hg, w_gate, w_branch, w_out, ln_g, ln_b, router_w, *, alpha, tm):
    T, D = x2d.shape
    bw = y_rg.shape[1]
    n_experts = 0 if router_w is None else router_w.shape[1]
    full = lambda shape: pl.BlockSpec(shape, lambda i: (0,) * len(shape))
    row = lambda width: pl.BlockSpec((tm, width), lambda i: (i, 0))
    in_specs = [row(D), row(bw), row(bw), row(bw), full(w_gate.shape), full(w_branch.shape),
                full(w_out.shape), full((1, D)), full((1, D))]
    args = [x2d, y_rg, y_gla, y_hg, w_gate, w_branch, w_out, ln_g, ln_b]
    out_specs = row(D)
    out_shape = jax.ShapeDtypeStruct((T, D), F32)
    if n_experts:
        rw = jnp.pad(router_w, ((0, 0), (0, V7X_LANES - n_experts)))
        rw_hi = rw.astype(MXU_DTYPE)
        rw = jnp.concatenate([rw_hi, (rw - rw_hi.astype(F32)).astype(MXU_DTYPE)], axis=1)
        in_specs.append(full(rw.shape))
        args.append(rw)
        per_row = D // V7X_LANES
        out_specs = (out_specs, row(V7X_LANES), pl.BlockSpec((tm * per_row, V7X_LANES), lambda i: (i, 0)))
        out_shape = (out_shape, jax.ShapeDtypeStruct((T, V7X_LANES), F32),
                     jax.ShapeDtypeStruct((T * per_row, V7X_LANES), F32))
    return pl.pallas_call(
        functools.partial(_merge_kernel, alpha=alpha, n_experts=n_experts),
        grid=(T // tm,),
        in_specs=in_specs,
        out_specs=out_specs,
        out_shape=out_shape,
        compiler_params=_cparams("parallel"),
        name="merge_router" if n_experts else "merge",
    )(*args)


def _ffn_kernel(x_ref, wg_ref, wu_ref, wd_ref, g_ref, b_ref, o_ref, xm_s, acc_s, *, alpha):
    f = pl.program_id(1)

    @pl.when(f == 0)
    def _():
        xm_s[...] = x_ref[...].astype(MXU_DTYPE)
        acc_s[...] = jnp.zeros_like(acc_s)

    xm = xm_s[...]
    hidden = jax.nn.silu(_mm(xm, wg_ref[...])) * _mm(xm, wu_ref[...])
    acc_s[...] += _mm(hidden, wd_ref[...])

    @pl.when(f == pl.num_programs(1) - 1)
    def _():
        o_ref[...] = _layer_norm(alpha * x_ref[...] + acc_s[...], g_ref[...], b_ref[...])


def _ffn(x2d, wg, wu, wd, ln_g, ln_b, *, alpha, tm, tf):
    T, D = x2d.shape
    F = wg.shape[1]
    return pl.pallas_call(
        functools.partial(_ffn_kernel, alpha=alpha),
        grid=(T // tm, F // tf),
        in_specs=[
            pl.BlockSpec((tm, D), lambda i, f: (i, 0)),
            pl.BlockSpec((D, tf), lambda i, f: (0, f)),
            pl.BlockSpec((D, tf), lambda i, f: (0, f)),
            pl.BlockSpec((tf, D), lambda i, f: (f, 0)),
            pl.BlockSpec((1, D), lambda i, f: (0, 0)),
            pl.BlockSpec((1, D), lambda i, f: (0, 0)),
        ],
        out_specs=pl.BlockSpec((tm, D), lambda i, f: (i, 0)),
        out_shape=jax.ShapeDtypeStruct((T, D), F32),
        scratch_shapes=[pltpu.VMEM((tm, D), MXU_DTYPE), pltpu.VMEM((tm, D), F32)],
        compiler_params=_cparams("parallel", "arbitrary"),
        name="ffn_swiglu",
    )(x2d, wg, wu, wd, ln_g, ln_b)


def _row_copy(src_hbm, src_row, dst, dst_row, sem):
    per_row = src_hbm.shape[1]
    return pltpu.make_async_copy(src_hbm.at[src_row], dst.at[pl.ds(dst_row * per_row, per_row), :], sem)


def _moe_kernel(be_ref, nb_ref, tok_ref, tok_next_ref, x_hbm, wg_ref, wu_ref, wd_ref, o_ref,
                xg_s, sem, xm_s, acc_s, *, bm, per_step):
    i = pl.program_id(0)
    f = pl.program_id(1)
    nf = pl.num_programs(1)
    n_used = nb_ref[0]
    slot = lax.rem(i, 2)
    d = xm_s.shape[1]

    def start_rows(t_ref, sl, step):
        for j in range(per_step):
            r = step * per_step + j
            _row_copy(x_hbm, t_ref[0, 0, jnp.minimum(r, bm - 1)], xg_s.at[sl], r, sem.at[sl]).start()

    @pl.when(jnp.logical_and(f == 0, i == 0))
    def _():
        def body(step, c):
            start_rows(tok_ref, 0, step)
            return c
        lax.fori_loop(0, nf, body, 0)

    @pl.when(jnp.logical_and(f == 0, i <= n_used))
    def _():
        for r in range(xg_s.shape[1] // x_hbm.shape[1]):
            _row_copy(x_hbm, 0, xg_s.at[slot], r, sem.at[slot]).wait()
        xm_s[...] = _load_row_tiles(xg_s.at[slot], 0, bm, d).astype(MXU_DTYPE)
        acc_s[...] = jnp.zeros_like(acc_s)

    @pl.when(i < n_used)
    def _():
        start_rows(tok_next_ref, 1 - slot, f)
        xm = xm_s[...]
        hidden = jax.nn.silu(_mm(xm, wg_ref[...])) * _mm(xm, wu_ref[...])
        acc_s[...] += _mm(hidden, wd_ref[...])

    last = f == nf - 1

    @pl.when(jnp.logical_and(last, i < n_used))
    def _():
        _store_row_tiles(o_ref, acc_s[...])

    @pl.when(jnp.logical_and(last, i >= n_used))
    def _():
        o_ref[...] = jnp.zeros_like(o_ref)


def _moe_experts(x_rows, block_expert, n_used, buf_tok, wg, wu, wd, *, bm, tf):
    _, per_row, _ = x_rows.shape
    E, D, F = wg.shape
    nb = buf_tok.shape[0]
    nf = F // tf
    per_step = pl.cdiv(bm, nf)

    def w_idx(i, f, be, nu):
        return jnp.where(i < nu[0], f, nf - 1)

    grid_spec = pltpu.PrefetchScalarGridSpec(
        num_scalar_prefetch=2,
        grid=(nb, nf),
        in_specs=[
            pl.BlockSpec((1, 1, bm), lambda i, f, be, nu: (i, 0, 0), memory_space=pltpu.SMEM),
            pl.BlockSpec((1, 1, bm), lambda i, f, be, nu: (jnp.minimum(i + 1, nb - 1), 0, 0),
                         memory_space=pltpu.SMEM),
            pl.BlockSpec(memory_space=pl.ANY),
            pl.BlockSpec((None, D, tf), lambda i, f, be, nu: (be[i], 0, w_idx(i, f, be, nu))),
            pl.BlockSpec((None, D, tf), lambda i, f, be, nu: (be[i], 0, w_idx(i, f, be, nu))),
            pl.BlockSpec((None, tf, D), lambda i, f, be, nu: (be[i], w_idx(i, f, be, nu), 0)),
        ],
        out_specs=pl.BlockSpec((bm * per_row, V7X_LANES), lambda i, f, be, nu: (i, 0)),
        scratch_shapes=[
            pltpu.VMEM((2, nf * per_step * per_row, V7X_LANES), F32),
            pltpu.SemaphoreType.DMA((2,)),
            pltpu.VMEM((bm, D), MXU_DTYPE),
            pltpu.VMEM((bm, D), F32),
        ],
    )
    return pl.pallas_call(
        functools.partial(_moe_kernel, bm=bm, per_step=per_step),
        grid_spec=grid_spec,
        out_shape=jax.ShapeDtypeStruct((nb * bm * per_row, V7X_LANES), F32),
        compiler_params=_cparams("arbitrary", "arbitrary"),
        name="moe_experts",
    )(block_expert, n_used, buf_tok, buf_tok, x_rows, wg, wu, wd)


def _combine_kernel(pos_ref, pos_next_ref, ys_hbm, x_ref, route_ref, g_ref, b_ref, o_ref, yg_s, sem,
                    *, tm, alpha):
    i = pl.program_id(0)
    n = pl.num_programs(0)
    slot = lax.rem(i, 2)
    rows = TOP_K * tm

    def wait_gather(sl):
        for r in range(rows):
            _row_copy(ys_hbm, 0, yg_s.at[sl], r, sem.at[sl]).wait()

    @pl.when(i == 0)
    def _():
        def body(r, c):
            _row_copy(ys_hbm, pos_ref[0, 0, r], yg_s.at[0], r, sem.at[0]).start()
            return c
        lax.fori_loop(0, rows, body, 0, unroll=8)

    wait_gather(slot)
    for r in range(rows):
        _row_copy(ys_hbm, pos_next_ref[0, 0, r], yg_s.at[1 - slot], r, sem.at[1 - slot]).start()
    route = route_ref[...]
    x = x_ref[...]
    y = alpha * x
    for k in range(TOP_K):
        y = y + route[:, TOP_K + k:TOP_K + k + 1] * _load_row_tiles(yg_s.at[slot], k * tm, tm, x.shape[1])
    o_ref[...] = _layer_norm(y, g_ref[...], b_ref[...])

    @pl.when(i == n - 1)
    def _():
        wait_gather(1 - slot)


def _moe_combine(ys_rows, pos, x2d, route, ln_g, ln_b, *, alpha, tm):
    T, D = x2d.shape
    n = T // tm
    per_row = ys_rows.shape[1]
    return pl.pallas_call(
        functools.partial(_combine_kernel, tm=tm, alpha=alpha),
        grid=(n,),
        in_specs=[
            pl.BlockSpec((1, 1, TOP_K * tm), lambda i: (i, 0, 0), memory_space=pltpu.SMEM),
            pl.BlockSpec((1, 1, TOP_K * tm), lambda i: (jnp.minimum(i + 1, n - 1), 0, 0),
                         memory_space=pltpu.SMEM),
            pl.BlockSpec(memory_space=pl.ANY),
            pl.BlockSpec((tm, D), lambda i: (i, 0)),
            pl.BlockSpec((tm, V7X_LANES), lambda i: (i, 0)),
            pl.BlockSpec((1, D), lambda i: (0, 0)),
            pl.BlockSpec((1, D), lambda i: (0, 0)),
        ],
        out_specs=pl.BlockSpec((tm, D), lambda i: (i, 0)),
        out_shape=jax.ShapeDtypeStruct((T, D), F32),
        scratch_shapes=[pltpu.VMEM((2, TOP_K * tm * per_row, V7X_LANES), F32), pltpu.SemaphoreType.DMA((2,))],
        compiler_params=_cparams("arbitrary"),
        name="moe_combine",
    )(pos, pos, ys_rows, x2d, route, ln_g, ln_b)


def _moe_layer(x2d, x_rows, route, wg, wu, wd, ln_g, ln_b, *, alpha, bm, tf, tm):
    T, D = x2d.shape
    E = wg.shape[0]
    tk = T * TOP_K
    per_row = D // V7X_LANES
    nb = tk // bm + E + 1
    expert = route[:, :TOP_K].astype(jnp.int32).reshape(tk)
    onehot = (expert[:, None] == jnp.arange(E, dtype=jnp.int32)[None, :]).astype(jnp.int32)
    csum = jnp.cumsum(onehot, axis=0)
    counts = csum[-1]
    rank = jnp.sum((csum - onehot) * onehot, axis=1)
    padded = (counts + bm - 1) // bm * bm
    pad_end = jnp.cumsum(padded)
    pad_start = pad_end - padded
    dest = jnp.sum(pad_start[None, :] * onehot, axis=1) + rank
    token = jnp.arange(tk, dtype=jnp.int32) // TOP_K
    buf_tok = jnp.zeros((nb * bm,), jnp.int32).at[dest].set(token, unique_indices=True, mode="promise_in_bounds")
    block_start = jnp.arange(nb, dtype=jnp.int32) * bm
    block_expert = jnp.minimum(jnp.searchsorted(pad_end, block_start, side="right"), E - 1).astype(jnp.int32)
    n_used = (pad_end[-1] // bm).astype(jnp.int32).reshape(1)
    ys = _moe_experts(x_rows.reshape(T, per_row, V7X_LANES), block_expert, n_used, buf_tok.reshape(nb, 1, bm),
                      wg, wu, wd, bm=bm, tf=tf)
    pos = dest.reshape(T // tm, tm, TOP_K).transpose(0, 2, 1).reshape(T // tm, 1, TOP_K * tm)
    return _moe_combine(ys.reshape(nb * bm, per_row, V7X_LANES), pos, x2d, route, ln_g, ln_b, alpha=alpha, tm=tm)


def _block_diag(w):
    g, n, _ = w.shape
    per = V7X_MXU_DIM // n
    w = w.reshape(g // per, per, n, n)
    eye = jnp.eye(per, dtype=w.dtype)
    return jnp.einsum("apij,pq->apiqj", w, eye).reshape(g // per, V7X_MXU_DIM, V7X_MXU_DIM)


def _pick(n, pref):
    t = min(n, pref)
    while n % t:
        t //= 2
    return t


def kernel(x, w_in, conv_w, conv_b, rg_wr, rg_br, rg_wi, rg_bi, rg_lambda, gla_wa, gla_ba, gla_norm_w, hg_lb_logits, hg_norm_w, w_branch, w_out, ln1_g, ln1_b, ln2_g, ln2_b, ffn_wg, ffn_wu, ffn_wd, router_w, moe_wg, moe_wu, moe_wd):
    B, S, D = x.shape
    depth = w_in.shape[0]
    T = B * S
    alpha = (2 * depth) ** 0.25
    rg_w = conv_w.shape[2]
    g_hk, g_hv = GLA_HEADS * GLA_DK, GLA_HEADS * GLA_DV
    rank = gla_wa.shape[1]
    h_hk, h_hv = HG_HEADS * HG_DK, HG_HEADS * HG_DV
    n_br = w_branch.shape[1]
    sizes = (rg_w, g_hk, g_hk, g_hv, rank, g_hv, h_hk, h_hk, h_hv, h_hv, n_br * D)
    offs = [0]
    for sz in sizes:
        offs.append(offs[-1] + sz)
    assert offs[-1] == w_in.shape[2]

    ts_rg = _pick(S, 512)
    ts_cla = _pick(S, 256)
    tm = _pick(T, 512)
    row2 = lambda v: v.reshape(1, -1)
    cast = lambda w: w.astype(MXU_DTYPE)

    x2d = x.reshape(T, D)
    for l in range(depth):
        wl = w_in[l]
        w_rg = cast(wl[:, offs[0]:offs[1]])
        a_cols = jnp.pad(wl[:, offs[4]:offs[5]], ((0, 0), (0, V7X_LANES - rank)))
        w_gla = cast(jnp.concatenate([wl[:, offs[1]:offs[4]], wl[:, offs[5]:offs[6]], a_cols], axis=1))
        w_hg = cast(wl[:, offs[6]:offs[10]])
        w_gate = cast(wl[:, offs[10]:offs[11]].reshape(D, n_br, D).transpose(1, 0, 2))
        wa = cast(jnp.pad(gla_wa[l], ((0, V7X_LANES - rank), (0, 0))))
        x3d = x2d.reshape(B, S, D)

        y_rg = _rg_branch(x3d, w_rg, conv_w[l], row2(conv_b[l]), cast(_block_diag(rg_wr[l])), row2(rg_br[l]),
                          cast(_block_diag(rg_wi[l])), row2(rg_bi[l]), row2(rg_lambda[l]), ts=ts_rg)
        y_gla = _gla_branch(x3d, w_gla, wa, row2(gla_ba[l]), row2(gla_norm_w[l]), ts=ts_cla)
        y_hg = _hg_branch(x3d, w_hg, hg_lb_logits, row2(hg_norm_w[l]), ts=ts_cla, layer=l)

        is_moe = l % 2 == 1
        j = l // 2
        merged = _merge(x2d, y_rg.reshape(T, -1), y_gla.reshape(T, -1), y_hg.reshape(T, -1), w_gate,
                        cast(w_branch[l]), cast(w_out[l]), row2(ln1_g[l]), row2(ln1_b[l]),
                        router_w[j] if is_moe else None, alpha=alpha, tm=tm)
        if is_moe:
            x1, route, x1_rows = merged
            x2d = _moe_layer(x1, x1_rows, route, moe_wg[j], moe_wu[j], moe_wd[j],
                             row2(ln2_g[l]), row2(ln2_b[l]), alpha=alpha,
                             bm=_pick(T * TOP_K, 1024), tf=_pick(moe_wg.shape[3], 512), tm=_pick(T, 256))
        else:
            x2d = _ffn(merged, ffn_wg[j], ffn_wu[j], ffn_wd[j], row2(ln2_g[l]), row2(ln2_b[l]),
                       alpha=alpha, tm=_pick(T, 1024), tf=_pick(ffn_wg.shape[2], 256))
    return x2d.reshape(B, S, D)
```

```python
import functools

import jax
import jax.numpy as jnp
from jax import lax
from jax.experimental import pallas as pl
from jax.experimental.pallas import tpu as pltpu

F32 = jnp.float32
MXU_DTYPE = jnp.bfloat16

RG_C = 8.0
GLA_HEADS, GLA_DK, GLA_DV = 4, 64, 128
GLA_GATE_NORMALIZER = 16.0
HG_HEADS, HG_DK, HG_DV = 4, 128, 128
HG_F_MIN = 1e-30
TOP_K = 2
LN_EPS = 1e-5
RMS_EPS = 1e-6

CHUNK = 64
SUB = 32
PAD = SUB
SAFE_EXP = 60.0

V7X_LANES = 128
V7X_SUBLANES = 8
V7X_MXU_DIM = 256
V7X_VMEM_BYTES = 64 * 1024 * 1024
VMEM_LIMIT = V7X_VMEM_BYTES * 7 // 8


def _cparams(*sem):
    return pltpu.CompilerParams(dimension_semantics=sem, vmem_limit_bytes=VMEM_LIMIT)


def _mm(a, b):
    return jnp.dot(a.astype(MXU_DTYPE), b.astype(MXU_DTYPE), preferred_element_type=F32)


def _mm_nt(a, b):
    return lax.dot_general(a.astype(MXU_DTYPE), b.astype(MXU_DTYPE), (((1,), (1,)), ((), ())),
                           preferred_element_type=F32)


def _mm_tn(a, b):
    return lax.dot_general(a.astype(MXU_DTYPE), b.astype(MXU_DTYPE), (((0,), (0,)), ((), ())),
                           preferred_element_type=F32)


def _softplus(z):
    return jnp.maximum(z, 0.0) + jnp.log1p(jnp.exp(-jnp.abs(z)))


def _log_sigmoid(z):
    return jnp.minimum(z, 0.0) - jnp.log1p(jnp.exp(-jnp.abs(z)))


def _store_row_tiles(ref, val):
    n, d = val.shape
    per_row = d // V7X_LANES
    for j in range(per_row):
        ref[pl.ds(j, n, stride=per_row), :] = val[:, j * V7X_LANES:(j + 1) * V7X_LANES]


def _load_row_tiles(ref, first_row, n, d):
    per_row = d // V7X_LANES
    return jnp.concatenate(
        [ref[pl.ds(first_row * per_row + j, n, stride=per_row), :] for j in range(per_row)], axis=1)


def _layer_norm(v, g, b):
    mu = jnp.mean(v, axis=-1, keepdims=True)
    d = v - mu
    var = jnp.mean(d * d, axis=-1, keepdims=True)
    return d * lax.rsqrt(var + LN_EPS) * g + b


def _rg_kernel(x_ref, w_ref, cw_ref, cb_ref, wr_ref, br_ref, wi_ref, bi_ref, lam_ref, y_ref,
               tail_ref, h_ref, *, ts, width, taps):
    @pl.when(pl.program_id(1) == 0)
    def _():
        tail_ref[...] = jnp.zeros_like(tail_ref)
        h_ref[...] = jnp.zeros_like(h_ref)

    u = _mm(x_ref[...], w_ref[...])
    ext = jnp.concatenate([tail_ref[...], u], axis=0)
    c = cb_ref[...]
    for j in range(taps):
        off = V7X_SUBLANES - (taps - 1) + j
        c = c + ext[off:off + ts] * cw_ref[j:j + 1, :]
    tail_ref[...] = u[ts - V7X_SUBLANES:]

    cm = c.astype(MXU_DTYPE)
    nblk = width // V7X_MXU_DIM
    zr = jnp.concatenate([_mm(cm[:, n * V7X_MXU_DIM:(n + 1) * V7X_MXU_DIM], wr_ref[n]) for n in range(nblk)], axis=1)
    zi = jnp.concatenate([_mm(cm[:, n * V7X_MXU_DIM:(n + 1) * V7X_MXU_DIM], wi_ref[n]) for n in range(nblk)], axis=1)
    r = jax.nn.sigmoid(zr + br_ref[...])
    i = jax.nn.sigmoid(zi + bi_ref[...])
    log_a = (-RG_C) * r * _softplus(-lam_ref[...])
    a = jnp.exp(log_a)
    b = jnp.sqrt(jnp.maximum(-jnp.tanh(log_a) * (a * a + 1.0), 0.0)) * (i * c)

    row = lax.broadcasted_iota(jnp.int32, (ts, width), 0)
    shift = 1
    while shift < ts:
        valid = row >= shift
        b = jnp.where(valid, a * pltpu.roll(b, shift, 0) + b, b)
        a = jnp.where(valid, a * pltpu.roll(a, shift, 0), a)
        shift *= 2
    h = b + a * h_ref[...]
    h_ref[...] = h[ts - 1:ts, :]
    y_ref[...] = h.astype(y_ref.dtype)


def _rg_branch(x, w, conv_w, conv_b, wr_bd, br, wi_bd, bi, lam, *, ts):
    B, S, D = x.shape
    width = w.shape[1]
    taps = conv_w.shape[0]
    nblk = width // V7X_MXU_DIM
    full = lambda shape: pl.BlockSpec(shape, lambda b, s: (0,) * len(shape))
    return pl.pallas_call(
        functools.partial(_rg_kernel, ts=ts, width=width, taps=taps),
        grid=(B, S // ts),
        in_specs=[
            pl.BlockSpec((None, ts, D), lambda b, s: (b, s, 0)),
            full((D, width)), full((taps, width)), full((1, width)),
            full((nblk, V7X_MXU_DIM, V7X_MXU_DIM)), full((1, width)),
            full((nblk, V7X_MXU_DIM, V7X_MXU_DIM)), full((1, width)), full((1, width)),
        ],
        out_specs=pl.BlockSpec((None, ts, width), lambda b, s: (b, s, 0)),
        out_shape=jax.ShapeDtypeStruct((B, S, width), MXU_DTYPE),
        scratch_shapes=[pltpu.VMEM((V7X_SUBLANES, width), F32), pltpu.VMEM((1, width), F32)],
        compiler_params=_cparams("parallel", "arbitrary"),
        name="rg_lru",
    )(x, w, conv_w, conv_b, wr_bd, br, wi_bd, bi, lam)


def _cla_scratch(ts, heads, dk, dv):
    hk, hv = heads * dk, heads * dv
    return [
        pltpu.VMEM((ts, hk), F32),
        pltpu.VMEM((PAD + ts, hk), F32),
        pltpu.VMEM((PAD + ts, hk), F32),
        pltpu.VMEM((PAD + ts, hv), F32),
        pltpu.VMEM((ts, hk), MXU_DTYPE),
        pltpu.VMEM((ts, hv), F32),
        pltpu.VMEM((heads, dv, dk), F32),
    ]


def _cla_core(q, k, lg, v, scratch, *, ts, heads, dk, dv):
    q_s, k_s, g_s, v_s, qg_s, o_s, st_ref = scratch
    hk = heads * dk
    nsub = CHUNK // SUB

    @pl.when(pl.program_id(1) == 0)
    def _():
        st_ref[...] = jnp.zeros_like(st_ref)
        k_s[0:PAD, :] = jnp.zeros((PAD, hk), F32)
        g_s[0:PAD, :] = jnp.zeros((PAD, hk), F32)
        v_s[0:PAD, :] = jnp.zeros((PAD, heads * dv), F32)

    row = lax.broadcasted_iota(jnp.int32, (ts, hk), 0)
    in_sub = row % SUB
    in_chunk = row % CHUNK
    g = lg
    shift = 1
    while shift < CHUNK:
        g = g + jnp.where(in_chunk >= shift, pltpu.roll(g, shift, 0), 0.0)
        shift *= 2
    span = g - jnp.where(in_chunk >= SUB, pltpu.roll(g, SUB, 0), 0.0)
    unsafe = jnp.max(-span) > SAFE_EXP

    q_s[...] = q
    k_s[PAD:PAD + ts, :] = k
    g_s[PAD:PAD + ts, :] = g
    v_s[PAD:PAD + ts, :] = v
    qg_s[...] = (q * jnp.exp(g)).astype(MXU_DTYPE)

    ci = lax.broadcasted_iota(jnp.int32, (CHUNK, CHUNK), 0)
    si = lax.broadcasted_iota(jnp.int32, (CHUNK, CHUNK), 1)
    col_limit = jnp.where(unsafe, (ci // SUB) * SUB - 1, ci)
    keep = si <= col_limit

    def chunk_body(n, carry):
        r0 = pl.multiple_of(n * CHUNK, CHUNK)
        gc = g_s[pl.ds(PAD + r0, CHUNK), :]
        kc = k_s[pl.ds(PAD + r0, CHUNK), :]
        vc = v_s[pl.ds(PAD + r0, CHUNK), :].astype(MXU_DTYPE)
        qc = q_s[pl.ds(r0, CHUNK), :]
        qgc = qg_s[pl.ds(r0, CHUNK), :]
        g_last = gc[CHUNK - 1:CHUNK, :]
        k_dec = (kc * jnp.exp(g_last - gc)).astype(MXU_DTYPE)
        dec = jnp.exp(g_last)
        q_sub, k_sub = [], []
        for i in range(nsub):
            rows = slice(SUB * i, SUB * (i + 1))
            ref = gc[SUB * i - 1:SUB * i, :] if i else jnp.zeros((1, hk), F32)
            q_sub.append((qc[rows] * jnp.exp(gc[rows] - ref)).astype(MXU_DTYPE))
            k_sub.append((kc * jnp.exp(jnp.minimum(ref - gc, SAFE_EXP))).astype(MXU_DTYPE))
        for h in range(heads):
            ks = slice(h * dk, (h + 1) * dk)
            vs = slice(h * dv, (h + 1) * dv)
            st = st_ref[h]
            o = _mm_nt(qgc[:, ks], st)
            a = jnp.concatenate([_mm_nt(q_sub[i][:, ks], k_sub[i][:, ks]) for i in range(nsub)], axis=0)
            a = jnp.where(keep, a, 0.0)
            o = o + _mm(a, vc[:, vs])
            o_s[pl.ds(r0, CHUNK), vs] = o
            st_ref[h] = st * dec[:, ks] + _mm_tn(vc[:, vs], k_dec[:, ks])
        return carry

    lax.fori_loop(0, ts // CHUNK, chunk_body, 0, unroll=True)

    @pl.when(unsafe)
    def _():
        g_t = g_s[PAD:PAD + ts, :]
        q_t = q_s[...]
        for d in range(SUB):
            g_sh = g_s[PAD - d:PAD - d + ts, :]
            k_sh = k_s[PAD - d:PAD - d + ts, :]
            w = jnp.where(in_sub >= d, q_t * k_sh * jnp.exp(jnp.minimum(g_t - g_sh, 0.0)), 0.0)
            for h in range(heads):
                coef = jnp.sum(w[:, h * dk:(h + 1) * dk], axis=1, keepdims=True)
                v_sh = v_s[PAD - d:PAD - d + ts, h * dv:(h + 1) * dv]
                o_s[:, h * dv:(h + 1) * dv] += coef * v_sh

    return o_s[...]


def _gla_kernel(x_ref, w_ref, wa_ref, ba_ref, nw_ref, y_ref, *scratch, ts):
    hk, hv = GLA_HEADS * GLA_DK, GLA_HEADS * GLA_DV
    proj = _mm(x_ref[...], w_ref[...])
    q = proj[:, :hk] * (GLA_DK ** -0.5)
    k = proj[:, hk:2 * hk]
    v = proj[:, 2 * hk:2 * hk + hv]
    gate = proj[:, 2 * hk + hv:2 * hk + 2 * hv]
    a_low = proj[:, 2 * hk + 2 * hv:]
    lg = _log_sigmoid(_mm(a_low, wa_ref[...]) + ba_ref[...]) * (1.0 / GLA_GATE_NORMALIZER)
    o = _cla_core(q, k, lg, v, scratch, ts=ts, heads=GLA_HEADS, dk=GLA_DK, dv=GLA_DV)
    outs = []
    for h in range(GLA_HEADS):
        oh = o[:, h * GLA_DV:(h + 1) * GLA_DV]
        ms = jnp.mean(oh * oh, axis=-1, keepdims=True)
        outs.append(oh * lax.rsqrt(ms + RMS_EPS) * nw_ref[...])
    y_ref[...] = (jnp.concatenate(outs, axis=1) * jax.nn.silu(gate)).astype(y_ref.dtype)


def _hg_kernel(x_ref, w_ref, lbl_ref, nw_ref, y_ref, *scratch, ts, layer):
    hk, hv = HG_HEADS * HG_DK, HG_HEADS * HG_DV
    proj = _mm(x_ref[...], w_ref[...])
    lbl = lbl_ref[...]
    e = jnp.exp(lbl - jnp.max(lbl, axis=0, keepdims=True))
    lb_w = e / jnp.sum(e, axis=0, keepdims=True)
    lb = jnp.zeros((1, hk), F32)
    for l in range(1, layer + 1):
        lb = lb + lb_w[l:l + 1, :]
    z = proj[:, hk:2 * hk]
    forget = lb + (1.0 - lb) * jax.nn.sigmoid(z)
    lg = jnp.log(jnp.maximum(forget, HG_F_MIN))
    k = (1.0 - lb) * jax.nn.sigmoid(-z)
    q = jax.nn.silu(proj[:, :hk]) * (HG_DK ** -0.5)
    v = proj[:, 2 * hk:2 * hk + hv]
    gate = proj[:, 2 * hk + hv:]
    o = _cla_core(q, k, lg, v, scratch, ts=ts, heads=HG_HEADS, dk=HG_DK, dv=HG_DV)
    ms = jnp.mean(o * o, axis=-1, keepdims=True)
    y_ref[...] = (o * lax.rsqrt(ms + RMS_EPS) * nw_ref[...] * jax.nn.sigmoid(gate)).astype(y_ref.dtype)


def _gla_branch(x, w, wa, ba, nw, *, ts):
    B, S, D = x.shape
    hv = GLA_HEADS * GLA_DV
    full = lambda shape: pl.BlockSpec(shape, lambda b, s: (0,) * len(shape))
    return pl.pallas_call(
        functools.partial(_gla_kernel, ts=ts),
        grid=(B, S // ts),
        in_specs=[pl.BlockSpec((None, ts, D), lambda b, s: (b, s, 0)),
                  full(w.shape), full(wa.shape), full(ba.shape), full(nw.shape)],
        out_specs=pl.BlockSpec((None, ts, hv), lambda b, s: (b, s, 0)),
        out_shape=jax.ShapeDtypeStruct((B, S, hv), MXU_DTYPE),
        scratch_shapes=_cla_scratch(ts, GLA_HEADS, GLA_DK, GLA_DV),
        compiler_params=_cparams("parallel", "arbitrary"),
        name="gla",
    )(x, w, wa, ba, nw)


def _hg_branch(x, w, lb_logits, nw, *, ts, layer):
    B, S, D = x.shape
    hv = HG_HEADS * HG_DV
    full = lambda shape: pl.BlockSpec(shape, lambda b, s: (0,) * len(shape))
    return pl.pallas_call(
        functools.partial(_hg_kernel, ts=ts, layer=layer),
        grid=(B, S // ts),
        in_specs=[pl.BlockSpec((None, ts, D), lambda b, s: (b, s, 0)),
                  full(w.shape), full(lb_logits.shape), full(nw.shape)],
        out_specs=pl.BlockSpec((None, ts, hv), lambda b, s: (b, s, 0)),
        out_shape=jax.ShapeDtypeStruct((B, S, hv), MXU_DTYPE),
        scratch_shapes=_cla_scratch(ts, HG_HEADS, HG_DK, HG_DV),
        compiler_params=_cparams("parallel", "arbitrary"),
        name="hgrn2",
    )(x, w, lb_logits, nw)


def _merge_kernel(x_ref, yr_ref, yg_ref, yh_ref, wgate_ref, wbr_ref, wo_ref, g_ref, b_ref, *rest,
                  alpha, n_experts):
    if n_experts:
        rw_ref, out_ref, route_ref, out_rows_ref = rest
    else:
        (out_ref,) = rest
    x = x_ref[...]
    xm = x.astype(MXU_DTYPE)
    mixed = None
    for n, y_ref in enumerate((yr_ref, yg_ref, yh_ref)):
        term = jax.nn.sigmoid(_mm(xm, wgate_ref[n])) * _mm(y_ref[...], wbr_ref[n])
        mixed = term if mixed is None else mixed + term
    out = _layer_norm(alpha * x + _mm(mixed, wo_ref[...]), g_ref[...], b_ref[...])
    out_ref[...] = out
    if n_experts:
        _store_row_tiles(out_rows_ref, out)
        x_hi = out.astype(MXU_DTYPE)
        x_lo = out - x_hi.astype(F32)
        rw = rw_ref[...]
        p_hi = _mm(x_hi, rw)
        logits = p_hi[:, :V7X_LANES] + p_hi[:, V7X_LANES:] + _mm(x_lo, rw[:, :V7X_LANES])
        lane = lax.broadcasted_iota(jnp.int32, logits.shape, 1)
        neg = jnp.float32(-jnp.inf)
        l1 = jnp.where(lane < n_experts, logits, neg)
        m1 = jnp.max(l1, axis=1, keepdims=True)
        i1 = jnp.min(jnp.where(l1 == m1, lane, V7X_LANES), axis=1, keepdims=True)
        l2 = jnp.where(lane == i1, neg, l1)
        m2 = jnp.max(l2, axis=1, keepdims=True)
        i2 = jnp.min(jnp.where(l2 == m2, lane, V7X_LANES), axis=1, keepdims=True)
        e2 = jnp.exp(m2 - m1)
        w1 = 1.0 / (1.0 + e2)
        w2 = e2 / (1.0 + e2)
        route = jnp.where(lane == 0, i1.astype(F32),
                          jnp.where(lane == 1, i2.astype(F32),
                                    jnp.where(lane == 2, w1, jnp.where(lane == 3, w2, 0.0))))
        route_ref[...] = route


def _merge(x2d, y_rg, y_gla, y_hg, w_gate, w_branch, w_out, ln_g, ln_b, router_w, *, alpha, tm):
    T, D = x2d.shape
    bw = y_rg.shape[1]
    n_experts = 0 if router_w is None else router_w.shape[1]
    full = lambda shape: pl.BlockSpec(shape, lambda i: (0,) * len(shape))
    row = lambda width: pl.BlockSpec((tm, width), lambda i: (i, 0))
    in_specs = [row(D), row(bw), row(bw), row(bw), full(w_gate.shape), full(w_branch.shape),
                full(w_out.shape), full((1, D)), full((1, D))]
    args = [x2d, y_rg, y_gla, y_hg, w_gate, w_branch, w_out, ln_g, ln_b]
    out_specs = row(D)
    out_shape = jax.ShapeDtypeStruct((T, D), F32)
    if n_experts:
        rw = jnp.pad(router_w, ((0, 0), (0, V7X_LANES - n_experts)))
        rw_hi = rw.astype(MXU_DTYPE)
        rw = jnp.concatenate([rw_hi, (rw - rw_hi.astype(F32)).astype(MXU_DTYPE)], axis=1)
        in_specs.append(full(rw.shape))
        args.append(rw)
        per_row = D // V7X_LANES
        out_specs = (out_specs, row(V7X_LANES), pl.BlockSpec((tm * per_row, V7X_LANES), lambda i: (i, 0)))
        out_shape = (out_shape, jax.ShapeDtypeStruct((T, V7X_LANES), F32),
                     jax.ShapeDtypeStruct((T * per_row, V7X_LANES), F32))
    return pl.pallas_call(
        functools.partial(_merge_kernel, alpha=alpha, n_experts=n_experts),
        grid=(T // tm,),
        in_specs=in_specs,
        out_specs=out_specs,
        out_shape=out_shape,
        compiler_params=_cparams("parallel"),
        name="merge_router" if n_experts else "merge",
    )(*args)


def _ffn_kernel(x_ref, wg_ref, wu_ref, wd_ref, g_ref, b_ref, o_ref, xm_s, acc_s, *, alpha):
    f = pl.program_id(1)

    @pl.when(f == 0)
    def _():
        xm_s[...] = x_ref[...].astype(MXU_DTYPE)
        acc_s[...] = jnp.zeros_like(acc_s)

    xm = xm_s[...]
    hidden = jax.nn.silu(_mm(xm, wg_ref[...])) * _mm(xm, wu_ref[...])
    acc_s[...] += _mm(hidden, wd_ref[...])

    @pl.when(f == pl.num_programs(1) - 1)
    def _():
        o_ref[...] = _layer_norm(alpha * x_ref[...] + acc_s[...], g_ref[...], b_ref[...])


def _ffn(x2d, wg, wu, wd, ln_g, ln_b, *, alpha, tm, tf):
    T, D = x2d.shape
    F = wg.shape[1]
    return pl.pallas_call(
        functools.partial(_ffn_kernel, alpha=alpha),
        grid=(T // tm, F // tf),
        in_specs=[
            pl.BlockSpec((tm, D), lambda i, f: (i, 0)),
            pl.BlockSpec((D, tf), lambda i, f: (0, f)),
            pl.BlockSpec((D, tf), lambda i, f: (0, f)),
            pl.BlockSpec((tf, D), lambda i, f: (f, 0)),
            pl.BlockSpec((1, D), lambda i, f: (0, 0)),
            pl.BlockSpec((1, D), lambda i, f: (0, 0)),
        ],
        out_specs=pl.BlockSpec((tm, D), lambda i, f: (i, 0)),
        out_shape=jax.ShapeDtypeStruct((T, D), F32),
        scratch_shapes=[pltpu.VMEM((tm, D), MXU_DTYPE), pltpu.VMEM((tm, D), F32)],
        compiler_params=_cparams("parallel", "arbitrary"),
        name="ffn_swiglu",
    )(x2d, wg, wu, wd, ln_g, ln_b)


def _row_copy(src_hbm, src_row, dst, dst_row, sem):
    per_row = src_hbm.shape[1]
    return pltpu.make_async_copy(src_hbm.at[src_row], dst.at[pl.ds(dst_row * per_row, per_row), :], sem)


def _moe_kernel(be_ref, nb_ref, tok_ref, tok_next_ref, x_hbm, wg_ref, wu_ref, wd_ref, o_ref,
                xg_s, sem, xm_s, acc_s, *, bm, per_step):
    i = pl.program_id(0)
    f = pl.program_id(1)
    nf = pl.num_programs(1)
    n_used = nb_ref[0]
    slot = lax.rem(i, 2)
    d = xm_s.shape[1]

    def start_rows(t_ref, sl, step):
        for j in range(per_step):
            r = step * per_step + j
            _row_copy(x_hbm, t_ref[0, 0, jnp.minimum(r, bm - 1)], xg_s.at[sl], r, sem.at[sl]).start()

    @pl.when(jnp.logical_and(f == 0, i == 0))
    def _():
        def body(step, c):
            start_rows(tok_ref, 0, step)
            return c
        lax.fori_loop(0, nf, body, 0)

    @pl.when(jnp.logical_and(f == 0, i <= n_used))
    def _():
        for r in range(xg_s.shape[1] // x_hbm.shape[1]):
            _row_copy(x_hbm, 0, xg_s.at[slot], r, sem.at[slot]).wait()
        xm_s[...] = _load_row_tiles(xg_s.at[slot], 0, bm, d).astype(MXU_DTYPE)
        acc_s[...] = jnp.zeros_like(acc_s)

    @pl.when(i < n_used)
    def _():
        start_rows(tok_next_ref, 1 - slot, f)
        xm = xm_s[...]
        hidden = jax.nn.silu(_mm(xm, wg_ref[...])) * _mm(xm, wu_ref[...])
        acc_s[...] += _mm(hidden, wd_ref[...])

    last = f == nf - 1

    @pl.when(jnp.logical_and(last, i < n_used))
    def _():
        _store_row_tiles(o_ref, acc_s[...])

    @pl.when(jnp.logical_and(last, i >= n_used))
    def _():
        o_ref[...] = jnp.zeros_like(o_ref)


def _moe_experts(x_rows, block_expert, n_used, buf_tok, wg, wu, wd, *, bm, tf):
    _, per_row, _ = x_rows.shape
    E, D, F = wg.shape
    nb = buf_tok.shape[0]
    nf = F // tf
    per_step = pl.cdiv(bm, nf)

    def w_idx(i, f, be, nu):
        return jnp.where(i < nu[0], f, nf - 1)

    grid_spec = pltpu.PrefetchScalarGridSpec(
        num_scalar_prefetch=2,
        grid=(nb, nf),
        in_specs=[
            pl.BlockSpec((1, 1, bm), lambda i, f, be, nu: (i, 0, 0), memory_space=pltpu.SMEM),
            pl.BlockSpec((1, 1, bm), lambda i, f, be, nu: (jnp.minimum(i + 1, nb - 1), 0, 0),
                         memory_space=pltpu.SMEM),
            pl.BlockSpec(memory_space=pl.ANY),
            pl.BlockSpec((None, D, tf), lambda i, f, be, nu: (be[i], 0, w_idx(i, f, be, nu))),
            pl.BlockSpec((None, D, tf), lambda i, f, be, nu: (be[i], 0, w_idx(i, f, be, nu))),
            pl.BlockSpec((None, tf, D), lambda i, f, be, nu: (be[i], w_idx(i, f, be, nu), 0)),
        ],
        out_specs=pl.BlockSpec((bm * per_row, V7X_LANES), lambda i, f, be, nu: (i, 0)),
        scratch_shapes=[
            pltpu.VMEM((2, nf * per_step * per_row, V7X_LANES), F32),
            pltpu.SemaphoreType.DMA((2,)),
            pltpu.VMEM((bm, D), MXU_DTYPE),
            pltpu.VMEM((bm, D), F32),
        ],
    )
    return pl.pallas_call(
        functools.partial(_moe_kernel, bm=bm, per_step=per_step),
        grid_spec=grid_spec,
        out_shape=jax.ShapeDtypeStruct((nb * bm * per_row, V7X_LANES), F32),
        compiler_params=_cparams("arbitrary", "arbitrary"),
        name="moe_experts",
    )(block_expert, n_used, buf_tok, buf_tok, x_rows, wg, wu, wd)


def _combine_kernel(pos_ref, pos_next_ref, ys_hbm, x_ref, route_ref, g_ref, b_ref, o_ref, yg_s, sem,
                    *, tm, alpha):
    i = pl.program_id(0)
    n = pl.num_programs(0)
    slot = lax.rem(i, 2)
    rows = TOP_K * tm

    def start_gather(p_ref, sl):
        def body(r, c):
            _row_copy(ys_hbm, p_ref[0, 0, r], yg_s.at[sl], r, sem.at[sl]).start()
            return c
        lax.fori_loop(0, rows, body, 0, unroll=8)

    @pl.when(i == 0)
    def _():
        start_gather(pos_ref, 0)

    @pl.when(i + 1 < n)
    def _():
        start_gather(pos_next_ref, 1 - slot)

    for r in range(rows):
        _row_copy(ys_hbm, 0, yg_s.at[slot], r, sem.at[slot]).wait()
    route = route_ref[...]
    x = x_ref[...]
    y = alpha * x
    for k in range(TOP_K):
        y = y + route[:, TOP_K + k:TOP_K + k + 1] * _load_row_tiles(yg_s.at[slot], k * tm, tm, x.shape[1])
    o_ref[...] = _layer_norm(y, g_ref[...], b_ref[...])


def _moe_combine(ys_rows, pos, x2d, route, ln_g, ln_b, *, alpha, tm):
    T, D = x2d.shape
    n = T // tm
    per_row = ys_rows.shape[1]
    return pl.pallas_call(
        functools.partial(_combine_kernel, tm=tm, alpha=alpha),
        grid=(n,),
        in_specs=[
            pl.BlockSpec((1, 1, TOP_K * tm), lambda i: (i, 0, 0), memory_space=pltpu.SMEM),
            pl.BlockSpec((1, 1, TOP_K * tm), lambda i: (jnp.minimum(i + 1, n - 1), 0, 0),
                         memory_space=pltpu.SMEM),
            pl.BlockSpec(memory_space=pl.ANY),
            pl.BlockSpec((tm, D), lambda i: (i, 0)),
            pl.BlockSpec((tm, V7X_LANES), lambda i: (i, 0)),
            pl.BlockSpec((1, D), lambda i: (0, 0)),
            pl.BlockSpec((1, D), lambda i: (0, 0)),
        ],
        out_specs=pl.BlockSpec((tm, D), lambda i: (i, 0)),
        out_shape=jax.ShapeDtypeStruct((T, D), F32),
        scratch_shapes=[pltpu.VMEM((2, TOP_K * tm * per_row, V7X_LANES), F32), pltpu.SemaphoreType.DMA((2,))],
        compiler_params=_cparams("arbitrary"),
        name="moe_combine",
    )(pos, pos, ys_rows, x2d, route, ln_g, ln_b)


def _moe_layer(x2d, x_rows, route, wg, wu, wd, ln_g, ln_b, *, alpha, bm, tf, tm):
    T, D = x2d.shape
    E = wg.shape[0]
    tk = T * TOP_K
    per_row = D // V7X_LANES
    nb = tk // bm + E + 1
    expert = route[:, :TOP_K].astype(jnp.int32).reshape(tk)
    onehot = (expert[:, None] == jnp.arange(E, dtype=jnp.int32)[None, :]).astype(jnp.int32)
    csum = jnp.cumsum(onehot, axis=0)
    counts = csum[-1]
    rank = jnp.sum((csum - onehot) * onehot, axis=1)
    padded = (counts + bm - 1) // bm * bm
    pad_end = jnp.cumsum(padded)
    pad_start = pad_end - padded
    dest = jnp.sum(pad_start[None, :] * onehot, axis=1) + rank
    token = jnp.arange(tk, dtype=jnp.int32) // TOP_K
    buf_tok = jnp.zeros((nb * bm,), jnp.int32).at[dest].set(token)
    block_start = jnp.arange(nb, dtype=jnp.int32) * bm
    block_expert = jnp.minimum(jnp.searchsorted(pad_end, block_start, side="right"), E - 1).astype(jnp.int32)
    n_used = (pad_end[-1] // bm).astype(jnp.int32).reshape(1)
    ys = _moe_experts(x_rows.reshape(T, per_row, V7X_LANES), block_expert, n_used, buf_tok.reshape(nb, 1, bm),
                      wg, wu, wd, bm=bm, tf=tf)
    pos = dest.reshape(T // tm, tm, TOP_K).transpose(0, 2, 1).reshape(T // tm, 1, TOP_K * tm)
    return _moe_combine(ys.reshape(nb * bm, per_row, V7X_LANES), pos, x2d, route, ln_g, ln_b, alpha=alpha, tm=tm)


def _block_diag(w):
    g, n, _ = w.shape
    per = V7X_MXU_DIM // n
    w = w.reshape(g // per, per, n, n)
    eye = jnp.eye(per, dtype=w.dtype)
    return jnp.einsum("apij,pq->apiqj", w, eye).reshape(g // per, V7X_MXU_DIM, V7X_MXU_DIM)


def _pick(n, pref):
    t = min(n, pref)
    while n % t:
        t //= 2
    return t


def _tiles(S, T, f_dense, f_expert):
    return dict(
        ts_rg=_pick(S, 512),
        ts_cla=_pick(S, 256),
        tm_merge=_pick(T, 512),
        tm_ffn=_pick(T, 1024), tf_ffn=_pick(f_dense, V7X_MXU_DIM),
        bm_moe=_pick(T * TOP_K, 1024), tf_moe=_pick(f_expert, 2 * V7X_MXU_DIM),
        tm_combine=_pick(T, 256),
    )


def kernel(x, w_in, conv_w, conv_b, rg_wr, rg_br, rg_wi, rg_bi, rg_lambda, gla_wa, gla_ba, gla_norm_w, hg_lb_logits, hg_norm_w, w_branch, w_out, ln1_g, ln1_b, ln2_g, ln2_b, ffn_wg, ffn_wu, ffn_wd, router_w, moe_wg, moe_wu, moe_wd):
    B, S, D = x.shape
    depth = w_in.shape[0]
    T = B * S
    alpha = (2 * depth) ** 0.25
    rg_w = conv_w.shape[2]
    g_hk, g_hv = GLA_HEADS * GLA_DK, GLA_HEADS * GLA_DV
    rank = gla_wa.shape[1]
    h_hk, h_hv = HG_HEADS * HG_DK, HG_HEADS * HG_DV
    n_br = w_branch.shape[1]
    sizes = (rg_w, g_hk, g_hk, g_hv, rank, g_hv, h_hk, h_hk, h_hv, h_hv, n_br * D)
    offs = [0]
    for sz in sizes:
        offs.append(offs[-1] + sz)
    assert offs[-1] == w_in.shape[2]

    t = _tiles(S, T, ffn_wg.shape[2], moe_wg.shape[3])
    row2 = lambda v: v.reshape(1, -1)
    cast = lambda w: w.astype(MXU_DTYPE)

    x2d = x.reshape(T, D)
    for l in range(depth):
        wl = w_in[l]
        w_rg = cast(wl[:, offs[0]:offs[1]])
        a_cols = jnp.pad(wl[:, offs[4]:offs[5]], ((0, 0), (0, V7X_LANES - rank)))
        w_gla = cast(jnp.concatenate([wl[:, offs[1]:offs[4]], wl[:, offs[5]:offs[6]], a_cols], axis=1))
        w_hg = cast(wl[:, offs[6]:offs[10]])
        w_gate = cast(wl[:, offs[10]:offs[11]].reshape(D, n_br, D).transpose(1, 0, 2))
        wa = cast(jnp.pad(gla_wa[l], ((0, V7X_LANES - rank), (0, 0))))
        x3d = x2d.reshape(B, S, D)

        y_rg = _rg_branch(x3d, w_rg, conv_w[l], row2(conv_b[l]), cast(_block_diag(rg_wr[l])), row2(rg_br[l]),
                          cast(_block_diag(rg_wi[l])), row2(rg_bi[l]), row2(rg_lambda[l]), ts=t["ts_rg"])
        y_gla = _gla_branch(x3d, w_gla, wa, row2(gla_ba[l]), row2(gla_norm_w[l]), ts=t["ts_cla"])
        y_hg = _hg_branch(x3d, w_hg, hg_lb_logits, row2(hg_norm_w[l]), ts=t["ts_cla"], layer=l)

        is_moe = l % 2 == 1
        j = l // 2
        merged = _merge(x2d, y_rg.reshape(T, -1), y_gla.reshape(T, -1), y_hg.reshape(T, -1), w_gate,
                        cast(w_branch[l]), cast(w_out[l]), row2(ln1_g[l]), row2(ln1_b[l]),
                        router_w[j] if is_moe else None, alpha=alpha, tm=t["tm_merge"])
        if is_moe:
            x1, route, x1_rows = merged
            x2d = _moe_layer(x1, x1_rows, route, moe_wg[j], moe_wu[j], moe_wd[j],
                             row2(ln2_g[l]), row2(ln2_b[l]), alpha=alpha,
                             bm=t["bm_moe"], tf=t["tf_moe"], tm=t["tm_combine"])
        else:
            x2d = _ffn(merged, ffn_wg[j], ffn_wu[j], ffn_wd[j], row2(ln2_g[l]), row2(ln2_b[l]),
                       alpha=alpha, tm=t["tm_ffn"], tf=t["tf_ffn"])
    return x2d.reshape(B, S, D)
```

```python
import functools

import jax
import jax.numpy as jnp
from jax import lax
from jax.experimental import pallas as pl
from jax.experimental.pallas import tpu as pltpu

F32 = jnp.float32
MXU_DTYPE = jnp.bfloat16

RG_C = 8.0
GLA_HEADS, GLA_DK, GLA_DV = 4, 64, 128
GLA_GATE_NORMALIZER = 16.0
HG_HEADS, HG_DK, HG_DV = 4, 128, 128
HG_F_MIN = 1e-30
TOP_K = 2
LN_EPS = 1e-5
RMS_EPS = 1e-6

CHUNK = 64
SUB = 32
PAD = SUB
SAFE_EXP = 60.0

V7X_LANES = 128
V7X_SUBLANES = 8
V7X_MXU_DIM = 256
V7X_VMEM_BYTES = 64 * 1024 * 1024
VMEM_LIMIT = V7X_VMEM_BYTES * 7 // 8


def _cparams(*sem):
    return pltpu.CompilerParams(dimension_semantics=sem, vmem_limit_bytes=VMEM_LIMIT)


def _mm(a, b):
    return jnp.dot(a.astype(MXU_DTYPE), b.astype(MXU_DTYPE), preferred_element_type=F32)


def _mm_nt(a, b):
    return lax.dot_general(a.astype(MXU_DTYPE), b.astype(MXU_DTYPE), (((1,), (1,)), ((), ())),
                           preferred_element_type=F32)


def _mm_tn(a, b):
    return lax.dot_general(a.astype(MXU_DTYPE), b.astype(MXU_DTYPE), (((0,), (0,)), ((), ())),
                           preferred_element_type=F32)


def _softplus(z):
    return jnp.maximum(z, 0.0) + jnp.log1p(jnp.exp(-jnp.abs(z)))


def _log_sigmoid(z):
    return jnp.minimum(z, 0.0) - jnp.log1p(jnp.exp(-jnp.abs(z)))


def _store_row_tiles(ref, val):
    n, d = val.shape
    per_row = d // V7X_LANES
    for j in range(per_row):
        ref[pl.ds(j, n, stride=per_row), :] = val[:, j * V7X_LANES:(j + 1) * V7X_LANES]


def _load_row_tiles(ref, first_row, n, d):
    per_row = d // V7X_LANES
    return jnp.concatenate(
        [ref[pl.ds(first_row * per_row + j, n, stride=per_row), :] for j in range(per_row)], axis=1)


def _layer_norm(v, g, b):
    mu = jnp.mean(v, axis=-1, keepdims=True)
    d = v - mu
    var = jnp.mean(d * d, axis=-1, keepdims=True)
    return d * lax.rsqrt(var + LN_EPS) * g + b


def _rg_init(tail_ref, h_ref):
    tail_ref[...] = jnp.zeros_like(tail_ref)
    h_ref[...] = jnp.zeros_like(h_ref)


def _rg_body(xm, w_ref, cw_ref, cb_ref, wr_ref, br_ref, wi_ref, bi_ref, lam_ref, y_ref, tail_ref, h_ref):
    ts = xm.shape[0]
    taps, width = cw_ref.shape
    u = _mm(xm, w_ref[...])
    ext = jnp.concatenate([tail_ref[...], u], axis=0)
    c = cb_ref[...]
    for j in range(taps):
        off = V7X_SUBLANES - (taps - 1) + j
        c = c + ext[off:off + ts] * cw_ref[j:j + 1, :]
    tail_ref[...] = u[ts - V7X_SUBLANES:]

    cm = c.astype(MXU_DTYPE)
    nblk = width // V7X_MXU_DIM
    zr = jnp.concatenate([_mm(cm[:, n * V7X_MXU_DIM:(n + 1) * V7X_MXU_DIM], wr_ref[n]) for n in range(nblk)], axis=1)
    zi = jnp.concatenate([_mm(cm[:, n * V7X_MXU_DIM:(n + 1) * V7X_MXU_DIM], wi_ref[n]) for n in range(nblk)], axis=1)
    r = jax.nn.sigmoid(zr + br_ref[...])
    i = jax.nn.sigmoid(zi + bi_ref[...])
    log_a = (-RG_C) * r * _softplus(-lam_ref[...])
    a = jnp.exp(log_a)
    b = jnp.sqrt(jnp.maximum(-jnp.tanh(log_a) * (a * a + 1.0), 0.0)) * (i * c)

    row = lax.broadcasted_iota(jnp.int32, (ts, width), 0)
    shift = 1
    while shift < ts:
        valid = row >= shift
        b = jnp.where(valid, a * pltpu.roll(b, shift, 0) + b, b)
        a = jnp.where(valid, a * pltpu.roll(a, shift, 0), a)
        shift *= 2
    h = b + a * h_ref[...]
    h_ref[...] = h[ts - 1:ts, :]
    y_ref[...] = h.astype(y_ref.dtype)


def _cla_scratch(ts, heads, dk, dv):
    hk, hv = heads * dk, heads * dv
    return [
        pltpu.VMEM((ts, hk), F32),
        pltpu.VMEM((PAD + ts, hk), F32),
        pltpu.VMEM((PAD + ts, hk), F32),
        pltpu.VMEM((PAD + ts, hv), F32),
        pltpu.VMEM((ts, hk), MXU_DTYPE),
        pltpu.VMEM((ts, hv), F32),
        pltpu.VMEM((heads, dv, dk), F32),
    ]


def _cla_init(scratch):
    _, k_s, g_s, v_s, _, _, st_ref = scratch
    st_ref[...] = jnp.zeros_like(st_ref)
    k_s[0:PAD, :] = jnp.zeros((PAD, k_s.shape[1]), F32)
    g_s[0:PAD, :] = jnp.zeros((PAD, g_s.shape[1]), F32)
    v_s[0:PAD, :] = jnp.zeros((PAD, v_s.shape[1]), F32)


def _cla_main(q, k, lg, v, scratch, *, heads, dk, dv):
    q_s, k_s, g_s, v_s, qg_s, o_s, st_ref = scratch
    ts = q.shape[0]
    hk = heads * dk
    nsub = CHUNK // SUB

    row = lax.broadcasted_iota(jnp.int32, (ts, hk), 0)
    in_chunk = row % CHUNK
    g = lg
    shift = 1
    while shift < CHUNK:
        g = g + jnp.where(in_chunk >= shift, pltpu.roll(g, shift, 0), 0.0)
        shift *= 2
    span = g - jnp.where(in_chunk >= SUB, pltpu.roll(g, SUB, 0), 0.0)
    unsafe = jnp.max(-span) > SAFE_EXP

    q_s[...] = q
    k_s[PAD:PAD + ts, :] = k
    g_s[PAD:PAD + ts, :] = g
    v_s[PAD:PAD + ts, :] = v
    qg_s[...] = (q * jnp.exp(g)).astype(MXU_DTYPE)

    ci = lax.broadcasted_iota(jnp.int32, (CHUNK, CHUNK), 0)
    si = lax.broadcasted_iota(jnp.int32, (CHUNK, CHUNK), 1)
    col_limit = jnp.where(unsafe, (ci // SUB) * SUB - 1, ci)
    keep = si <= col_limit

    def chunk_body(n, carry):
        r0 = pl.multiple_of(n * CHUNK, CHUNK)
        gc = g_s[pl.ds(PAD + r0, CHUNK), :]
        kc = k_s[pl.ds(PAD + r0, CHUNK), :]
        vc = v_s[pl.ds(PAD + r0, CHUNK), :].astype(MXU_DTYPE)
        qc = q_s[pl.ds(r0, CHUNK), :]
        qgc = qg_s[pl.ds(r0, CHUNK), :]
        g_last = gc[CHUNK - 1:CHUNK, :]
        k_dec = (kc * jnp.exp(g_last - gc)).astype(MXU_DTYPE)
        dec = jnp.exp(g_last)
        q_sub, k_sub = [], []
        for i in range(nsub):
            rows = slice(SUB * i, SUB * (i + 1))
            ref = gc[SUB * i - 1:SUB * i, :] if i else jnp.zeros((1, hk), F32)
            q_sub.append((qc[rows] * jnp.exp(gc[rows] - ref)).astype(MXU_DTYPE))
            k_sub.append((kc * jnp.exp(jnp.minimum(ref - gc, SAFE_EXP))).astype(MXU_DTYPE))
        for h in range(heads):
            ks = slice(h * dk, (h + 1) * dk)
            vs = slice(h * dv, (h + 1) * dv)
            st = st_ref[h]
            o = _mm_nt(qgc[:, ks], st)
            a = jnp.concatenate([_mm_nt(q_sub[i][:, ks], k_sub[i][:, ks]) for i in range(nsub)], axis=0)
            a = jnp.where(keep, a, 0.0)
            o = o + _mm(a, vc[:, vs])
            o_s[pl.ds(r0, CHUNK), vs] = o
            st_ref[h] = st * dec[:, ks] + _mm_tn(vc[:, vs], k_dec[:, ks])
        return carry

    lax.fori_loop(0, ts // CHUNK, chunk_body, 0, unroll=True)
    return unsafe


def _cla_direct_diagonal(scratch, *, heads, dk, dv):
    q_s, k_s, g_s, v_s, _, o_s, _ = scratch
    ts = q_s.shape[0]
    in_sub = lax.broadcasted_iota(jnp.int32, q_s.shape, 0) % SUB
    g_t = g_s[PAD:PAD + ts, :]
    q_t = q_s[...]
    for d in range(SUB):
        g_sh = g_s[PAD - d:PAD - d + ts, :]
        k_sh = k_s[PAD - d:PAD - d + ts, :]
        w = jnp.where(in_sub >= d, q_t * k_sh * jnp.exp(jnp.minimum(g_t - g_sh, 0.0)), 0.0)
        for h in range(heads):
            coef = jnp.sum(w[:, h * dk:(h + 1) * dk], axis=1, keepdims=True)
            v_sh = v_s[PAD - d:PAD - d + ts, h * dv:(h + 1) * dv]
            o_s[:, h * dv:(h + 1) * dv] += coef * v_sh


def _gla_inputs(xm, w_ref, wa_ref, ba_ref):
    hk, hv = GLA_HEADS * GLA_DK, GLA_HEADS * GLA_DV
    proj = _mm(xm, w_ref[...])
    q = proj[:, :hk] * (GLA_DK ** -0.5)
    k = proj[:, hk:2 * hk]
    v = proj[:, 2 * hk:2 * hk + hv]
    gate = proj[:, 2 * hk + hv:2 * hk + 2 * hv]
    a_low = proj[:, 2 * hk + 2 * hv:]
    lg = _log_sigmoid(_mm(a_low, wa_ref[...]) + ba_ref[...]) * (1.0 / GLA_GATE_NORMALIZER)
    return q, k, lg, v, gate


def _gla_output(o, gate, nw_ref, y_ref):
    outs = []
    for h in range(GLA_HEADS):
        oh = o[:, h * GLA_DV:(h + 1) * GLA_DV]
        ms = jnp.mean(oh * oh, axis=-1, keepdims=True)
        outs.append(oh * lax.rsqrt(ms + RMS_EPS) * nw_ref[...])
    y_ref[...] = (jnp.concatenate(outs, axis=1) * jax.nn.silu(gate)).astype(y_ref.dtype)


def _hg_inputs(xm, w_ref, lbl_ref, layer):
    hk, hv = HG_HEADS * HG_DK, HG_HEADS * HG_DV
    proj = _mm(xm, w_ref[...])
    lbl = lbl_ref[...]
    e = jnp.exp(lbl - jnp.max(lbl, axis=0, keepdims=True))
    lb_w = e / jnp.sum(e, axis=0, keepdims=True)
    lb = jnp.zeros((1, hk), F32)
    for l in range(1, layer + 1):
        lb = lb + lb_w[l:l + 1, :]
    z = proj[:, hk:2 * hk]
    forget = lb + (1.0 - lb) * jax.nn.sigmoid(z)
    lg = jnp.log(jnp.maximum(forget, HG_F_MIN))
    k = (1.0 - lb) * jax.nn.sigmoid(-z)
    q = jax.nn.silu(proj[:, :hk]) * (HG_DK ** -0.5)
    v = proj[:, 2 * hk:2 * hk + hv]
    gate = proj[:, 2 * hk + hv:]
    return q, k, lg, v, gate


def _hg_output(o, gate, nw_ref, y_ref):
    ms = jnp.mean(o * o, axis=-1, keepdims=True)
    y_ref[...] = (o * lax.rsqrt(ms + RMS_EPS) * nw_ref[...] * jax.nn.sigmoid(gate)).astype(y_ref.dtype)


N_RG_PARAMS, N_GLA_PARAMS, N_HG_PARAMS, N_CLA_SCRATCH = 8, 4, 3, 7


def _mixer_kernel(x_ref, *refs, layer):
    take = lambda n: tuple(next(it) for _ in range(n))
    it = iter(refs)
    rg_p, gla_p, hg_p = take(N_RG_PARAMS), take(N_GLA_PARAMS), take(N_HG_PARAMS)
    y_rg, y_gla, y_hg = take(3)
    rg_s, gla_s, hg_s = take(2), take(N_CLA_SCRATCH), take(N_CLA_SCRATCH)
    gla_dims = dict(heads=GLA_HEADS, dk=GLA_DK, dv=GLA_DV)
    hg_dims = dict(heads=HG_HEADS, dk=HG_DK, dv=HG_DV)

    @pl.when(pl.program_id(1) == 0)
    def _():
        _rg_init(*rg_s)
        _cla_init(gla_s)
        _cla_init(hg_s)

    xm = x_ref[...].astype(MXU_DTYPE)
    _rg_body(xm, *rg_p, y_rg, *rg_s)
    gq, gk, glg, gv, g_gate = _gla_inputs(xm, *gla_p[:3])
    gla_unsafe = _cla_main(gq, gk, glg, gv, gla_s, **gla_dims)
    hq, hk, hlg, hv, h_gate = _hg_inputs(xm, *hg_p[:2], layer)
    hg_unsafe = _cla_main(hq, hk, hlg, hv, hg_s, **hg_dims)

    @pl.when(gla_unsafe)
    def _():
        _cla_direct_diagonal(gla_s, **gla_dims)

    @pl.when(hg_unsafe)
    def _():
        _cla_direct_diagonal(hg_s, **hg_dims)

    _gla_output(gla_s[5][...], g_gate, gla_p[3], y_gla)
    _hg_output(hg_s[5][...], h_gate, hg_p[2], y_hg)


def _mixers(x, rg_params, gla_params, hg_params, *, ts, layer):
    B, S, D = x.shape
    params = (*rg_params, *gla_params, *hg_params)
    assert (len(rg_params), len(gla_params), len(hg_params)) == (N_RG_PARAMS, N_GLA_PARAMS, N_HG_PARAMS)
    width = rg_params[1].shape[1]
    widths = (width, GLA_HEADS * GLA_DV, HG_HEADS * HG_DV)
    full = lambda shape: pl.BlockSpec(shape, lambda b, s: (0,) * len(shape))
    tile = lambda w: pl.BlockSpec((None, ts, w), lambda b, s: (b, s, 0))
    return pl.pallas_call(
        functools.partial(_mixer_kernel, layer=layer),
        grid=(B, S // ts),
        in_specs=[tile(D)] + [full(p.shape) for p in params],
        out_specs=tuple(tile(w) for w in widths),
        out_shape=tuple(jax.ShapeDtypeStruct((B, S, w), MXU_DTYPE) for w in widths),
        scratch_shapes=[pltpu.VMEM((V7X_SUBLANES, width), F32), pltpu.VMEM((1, width), F32)]
        + _cla_scratch(ts, GLA_HEADS, GLA_DK, GLA_DV) + _cla_scratch(ts, HG_HEADS, HG_DK, HG_DV),
        compiler_params=_cparams("parallel", "arbitrary"),
        name="mixers",
    )(x, *params)


def _merge_kernel(x_ref, yr_ref, yg_ref, yh_ref, wgate_ref, wbr_ref, wo_ref, g_ref, b_ref, *rest,
                  alpha, n_experts):
    if n_experts:
        rw_ref, out_ref, route_ref, out_rows_ref = rest
    else:
        (out_ref,) = rest
    x = x_ref[...]
    xm = x.astype(MXU_DTYPE)
    mixed = None
    for n, y_ref in enumerate((yr_ref, yg_ref, yh_ref)):
        term = jax.nn.sigmoid(_mm(xm, wgate_ref[n])) * _mm(y_ref[...], wbr_ref[n])
        mixed = term if mixed is None else mixed + term
    out = _layer_norm(alpha * x + _mm(mixed, wo_ref[...]), g_ref[...], b_ref[...])
    out_ref[...] = out
    if n_experts:
        _store_row_tiles(out_rows_ref, out)
        x_hi = out.astype(MXU_DTYPE)
        x_lo = out - x_hi.astype(F32)
        rw = rw_ref[...]
        p_hi = _mm(x_hi, rw)
        logits = p_hi[:, :V7X_LANES] + p_hi[:, V7X_LANES:] + _mm(x_lo, rw[:, :V7X_LANES])
        lane = lax.broadcasted_iota(jnp.int32, logits.shape, 1)
        neg = jnp.float32(-jnp.inf)
        l1 = jnp.where(lane < n_experts, logits, neg)
        m1 = jnp.max(l1, axis=1, keepdims=True)
        i1 = jnp.min(jnp.where(l1 == m1, lane, V7X_LANES), axis=1, keepdims=True)
        l2 = jnp.where(lane == i1, neg, l1)
        m2 = jnp.max(l2, axis=1, keepdims=True)
        i2 = jnp.min(jnp.where(l2 == m2, lane, V7X_LANES), axis=1, keepdims=True)
        e2 = jnp.exp(m2 - m1)
        w1 = 1.0 / (1.0 + e2)
        w2 = e2 / (1.0 + e2)
        route = jnp.where(lane == 0, i1.astype(F32),
                          jnp.where(lane == 1, i2.astype(F32),
                                    jnp.where(lane == 2, w1, jnp.where(lane == 3, w2, 0.0))))
        route_ref[...] = route


def _merge(x2d, y_rg, y_gla, y_hg, w_gate, w_branch, w_out, ln_g, ln_b, router_w, *, alpha, tm):
    T, D = x2d.shape
    bw = y_rg.shape[1]
    n_experts = 0 if router_w is None else router_w.shape[1]
    full = lambda shape: pl.BlockSpec(shape, lambda i: (0,) * len(shape))
    row = lambda width: pl.BlockSpec((tm, width), lambda i: (i, 0))
    in_specs = [row(D), row(bw), row(bw), row(bw), full(w_gate.shape), full(w_branch.shape),
                full(w_out.shape), full((1, D)), full((1, D))]
    args = [x2d, y_rg, y_gla, y_hg, w_gate, w_branch, w_out, ln_g, ln_b]
    out_specs = row(D)
    out_shape = jax.ShapeDtypeStruct((T, D), F32)
    if n_experts:
        rw = jnp.pad(router_w, ((0, 0), (0, V7X_LANES - n_experts)))
        rw_hi = rw.astype(MXU_DTYPE)
        rw = jnp.concatenate([rw_hi, (rw - rw_hi.astype(F32)).astype(MXU_DTYPE)], axis=1)
        in_specs.append(full(rw.shape))
        args.append(rw)
        per_row = D // V7X_LANES
        out_specs = (out_specs, row(V7X_LANES), pl.BlockSpec((tm * per_row, V7X_LANES), lambda i: (i, 0)))
        out_shape = (out_shape, jax.ShapeDtypeStruct((T, V7X_LANES), F32),
                     jax.ShapeDtypeStruct((T * per_row, V7X_LANES), F32))
    return pl.pallas_call(
        functools.partial(_merge_kernel, alpha=alpha, n_experts=n_experts),
        grid=(T // tm,),
        in_specs=in_specs,
        out_specs=out_specs,
        out_shape=out_shape,
        compiler_params=_cparams("parallel"),
        name="merge_router" if n_experts else "merge",
    )(*args)


def _ffn_kernel(x_ref, wg_ref, wu_ref, wd_ref, g_ref, b_ref, o_ref, xm_s, acc_s, *, alpha):
    f = pl.program_id(1)

    @pl.when(f == 0)
    def _():
        xm_s[...] = x_ref[...].astype(MXU_DTYPE)
        acc_s[...] = jnp.zeros_like(acc_s)

    xm = xm_s[...]
    hidden = jax.nn.silu(_mm(xm, wg_ref[...])) * _mm(xm, wu_ref[...])
    acc_s[...] += _mm(hidden, wd_ref[...])

    @pl.when(f == pl.num_programs(1) - 1)
    def _():
        o_ref[...] = _layer_norm(alpha * x_ref[...] + acc_s[...], g_ref[...], b_ref[...])


def _ffn(x2d, wg, wu, wd, ln_g, ln_b, *, alpha, tm, tf):
    T, D = x2d.shape
    F = wg.shape[1]
    return pl.pallas_call(
        functools.partial(_ffn_kernel, alpha=alpha),
        grid=(T // tm, F // tf),
        in_specs=[
            pl.BlockSpec((tm, D), lambda i, f: (i, 0)),
            pl.BlockSpec((D, tf), lambda i, f: (0, f)),
            pl.BlockSpec((D, tf), lambda i, f: (0, f)),
            pl.BlockSpec((tf, D), lambda i, f: (f, 0)),
            pl.BlockSpec((1, D), lambda i, f: (0, 0)),
            pl.BlockSpec((1, D), lambda i, f: (0, 0)),
        ],
        out_specs=pl.BlockSpec((tm, D), lambda i, f: (i, 0)),
        out_shape=jax.ShapeDtypeStruct((T, D), F32),
        scratch_shapes=[pltpu.VMEM((tm, D), MXU_DTYPE), pltpu.VMEM((tm, D), F32)],
        compiler_params=_cparams("parallel", "arbitrary"),
        name="ffn_swiglu",
    )(x2d, wg, wu, wd, ln_g, ln_b)


def _row_copy(src_hbm, src_row, dst, dst_row, sem):
    per_row = src_hbm.shape[1]
    return pltpu.make_async_copy(src_hbm.at[src_row], dst.at[pl.ds(dst_row * per_row, per_row), :], sem)


def _moe_kernel(be_ref, nb_ref, tok_ref, tok_next_ref, x_hbm, wg_ref, wu_ref, wd_ref, o_ref,
                xg_s, sem, xm_s, acc_s, *, bm, per_step):
    i = pl.program_id(0)
    f = pl.program_id(1)
    nf = pl.num_programs(1)
    n_used = nb_ref[0]
    slot = lax.rem(i, 2)
    d = xm_s.shape[1]

    def start_rows(t_ref, sl, step):
        for j in range(per_step):
            r = step * per_step + j
            _row_copy(x_hbm, t_ref[0, 0, jnp.minimum(r, bm - 1)], xg_s.at[sl], r, sem.at[sl]).start()

    @pl.when(jnp.logical_and(f == 0, i == 0))
    def _():
        def body(step, c):
            start_rows(tok_ref, 0, step)
            return c
        lax.fori_loop(0, nf, body, 0)

    @pl.when(jnp.logical_and(f == 0, i <= n_used))
    def _():
        for r in range(xg_s.shape[1] // x_hbm.shape[1]):
            _row_copy(x_hbm, 0, xg_s.at[slot], r, sem.at[slot]).wait()
        xm_s[...] = _load_row_tiles(xg_s.at[slot], 0, bm, d).astype(MXU_DTYPE)
        acc_s[...] = jnp.zeros_like(acc_s)

    @pl.when(i < n_used)
    def _():
        start_rows(tok_next_ref, 1 - slot, f)
        xm = xm_s[...]
        hidden = jax.nn.silu(_mm(xm, wg_ref[...])) * _mm(xm, wu_ref[...])
        acc_s[...] += _mm(hidden, wd_ref[...])

    last = f == nf - 1

    @pl.when(jnp.logical_and(last, i < n_used))
    def _():
        _store_row_tiles(o_ref, acc_s[...])

    @pl.when(jnp.logical_and(last, i >= n_used))
    def _():
        o_ref[...] = jnp.zeros_like(o_ref)


def _moe_experts(x_rows, block_expert, n_used, buf_tok, wg, wu, wd, *, bm, tf):
    _, per_row, _ = x_rows.shape
    E, D, F = wg.shape
    nb = buf_tok.shape[0]
    nf = F // tf
    per_step = pl.cdiv(bm, nf)

    def w_idx(i, f, be, nu):
        return jnp.where(i < nu[0], f, nf - 1)

    grid_spec = pltpu.PrefetchScalarGridSpec(
        num_scalar_prefetch=2,
        grid=(nb, nf),
        in_specs=[
            pl.BlockSpec((1, 1, bm), lambda i, f, be, nu: (i, 0, 0), memory_space=pltpu.SMEM),
            pl.BlockSpec((1, 1, bm), lambda i, f, be, nu: (jnp.minimum(i + 1, nb - 1), 0, 0),
                         memory_space=pltpu.SMEM),
            pl.BlockSpec(memory_space=pl.ANY),
            pl.BlockSpec((None, D, tf), lambda i, f, be, nu: (be[i], 0, w_idx(i, f, be, nu))),
            pl.BlockSpec((None, D, tf), lambda i, f, be, nu: (be[i], 0, w_idx(i, f, be, nu))),
            pl.BlockSpec((None, tf, D), lambda i, f, be, nu: (be[i], w_idx(i, f, be, nu), 0)),
        ],
        out_specs=pl.BlockSpec((bm * per_row, V7X_LANES), lambda i, f, be, nu: (i, 0)),
        scratch_shapes=[
            pltpu.VMEM((2, nf * per_step * per_row, V7X_LANES), F32),
            pltpu.SemaphoreType.DMA((2,)),
            pltpu.VMEM((bm, D), MXU_DTYPE),
            pltpu.VMEM((bm, D), F32),
        ],
    )
    return pl.pallas_call(
        functools.partial(_moe_kernel, bm=bm, per_step=per_step),
        grid_spec=grid_spec,
        out_shape=jax.ShapeDtypeStruct((nb * bm * per_row, V7X_LANES), F32),
        compiler_params=_cparams("arbitrary", "arbitrary"),
        name="moe_experts",
    )(block_expert, n_used, buf_tok, buf_tok, x_rows, wg, wu, wd)


def _combine_kernel(pos_ref, pos_next_ref, ys_hbm, x_ref, route_ref, g_ref, b_ref, o_ref, yg_s, sem,
                    *, tm, alpha):
    i = pl.program_id(0)
    n = pl.num_programs(0)
    slot = lax.rem(i, 2)
    rows = TOP_K * tm

    def start_gather(p_ref, sl):
        def body(r, c):
            _row_copy(ys_hbm, p_ref[0, 0, r], yg_s.at[sl], r, sem.at[sl]).start()
            return c
        lax.fori_loop(0, rows, body, 0, unroll=8)

    @pl.when(i == 0)
    def _():
        start_gather(pos_ref, 0)

    @pl.when(i + 1 < n)
    def _():
        start_gather(pos_next_ref, 1 - slot)

    for r in range(rows):
        _row_copy(ys_hbm, 0, yg_s.at[slot], r, sem.at[slot]).wait()
    route = route_ref[...]
    x = x_ref[...]
    y = alpha * x
    for k in range(TOP_K):
        y = y + route[:, TOP_K + k:TOP_K + k + 1] * _load_row_tiles(yg_s.at[slot], k * tm, tm, x.shape[1])
    o_ref[...] = _layer_norm(y, g_ref[...], b_ref[...])


def _moe_combine(ys_rows, pos, x2d, route, ln_g, ln_b, *, alpha, tm):
    T, D = x2d.shape
    n = T // tm
    per_row = ys_rows.shape[1]
    return pl.pallas_call(
        functools.partial(_combine_kernel, tm=tm, alpha=alpha),
        grid=(n,),
        in_specs=[
            pl.BlockSpec((1, 1, TOP_K * tm), lambda i: (i, 0, 0), memory_space=pltpu.SMEM),
            pl.BlockSpec((1, 1, TOP_K * tm), lambda i: (jnp.minimum(i + 1, n - 1), 0, 0),
                         memory_space=pltpu.SMEM),
            pl.BlockSpec(memory_space=pl.ANY),
            pl.BlockSpec((tm, D), lambda i: (i, 0)),
            pl.BlockSpec((tm, V7X_LANES), lambda i: (i, 0)),
            pl.BlockSpec((1, D), lambda i: (0, 0)),
            pl.BlockSpec((1, D), lambda i: (0, 0)),
        ],
        out_specs=pl.BlockSpec((tm, D), lambda i: (i, 0)),
        out_shape=jax.ShapeDtypeStruct((T, D), F32),
        scratch_shapes=[pltpu.VMEM((2, TOP_K * tm * per_row, V7X_LANES), F32), pltpu.SemaphoreType.DMA((2,))],
        compiler_params=_cparams("arbitrary"),
        name="moe_combine",
    )(pos, pos, ys_rows, x2d, route, ln_g, ln_b)


def _moe_layer(x2d, x_rows, route, wg, wu, wd, ln_g, ln_b, *, alpha, bm, tf, tm):
    T, D = x2d.shape
    E = wg.shape[0]
    tk = T * TOP_K
    per_row = D // V7X_LANES
    nb = tk // bm + E + 1
    expert = route[:, :TOP_K].astype(jnp.int32).reshape(tk)
    onehot = (expert[:, None] == jnp.arange(E, dtype=jnp.int32)[None, :]).astype(jnp.int32)
    csum = jnp.cumsum(onehot, axis=0)
    counts = csum[-1]
    rank = jnp.sum((csum - onehot) * onehot, axis=1)
    padded = (counts + bm - 1) // bm * bm
    pad_end = jnp.cumsum(padded)
    pad_start = pad_end - padded
    dest = jnp.sum(pad_start[None, :] * onehot, axis=1) + rank
    token = jnp.arange(tk, dtype=jnp.int32) // TOP_K
    buf_tok = jnp.zeros((nb * bm,), jnp.int32).at[dest].set(token)
    block_start = jnp.arange(nb, dtype=jnp.int32) * bm
    block_expert = jnp.minimum(jnp.searchsorted(pad_end, block_start, side="right"), E - 1).astype(jnp.int32)
    n_used = (pad_end[-1] // bm).astype(jnp.int32).reshape(1)
    ys = _moe_experts(x_rows.reshape(T, per_row, V7X_LANES), block_expert, n_used, buf_tok.reshape(nb, 1, bm),
                      wg, wu, wd, bm=bm, tf=tf)
    pos = dest.reshape(T // tm, tm, TOP_K).transpose(0, 2, 1).reshape(T // tm, 1, TOP_K * tm)
    return _moe_combine(ys.reshape(nb * bm, per_row, V7X_LANES), pos, x2d, route, ln_g, ln_b, alpha=alpha, tm=tm)


def _block_diag(w):
    g, n, _ = w.shape
    per = V7X_MXU_DIM // n
    w = w.reshape(g // per, per, n, n)
    eye = jnp.eye(per, dtype=w.dtype)
    return jnp.einsum("apij,pq->apiqj", w, eye).reshape(g // per, V7X_MXU_DIM, V7X_MXU_DIM)


def _pick(n, pref):
    t = min(n, pref)
    while n % t:
        t //= 2
    return t


def _tiles(S, T, f_dense, f_expert):
    return dict(
        ts_mix=_pick(S, 256),
        tm_merge=_pick(T, 512),
        tm_ffn=_pick(T, 1024), tf_ffn=_pick(f_dense, V7X_MXU_DIM),
        bm_moe=_pick(T * TOP_K, 1024), tf_moe=_pick(f_expert, 2 * V7X_MXU_DIM),
        tm_combine=_pick(T, 256),
    )


def kernel(x, w_in, conv_w, conv_b, rg_wr, rg_br, rg_wi, rg_bi, rg_lambda, gla_wa, gla_ba, gla_norm_w, hg_lb_logits, hg_norm_w, w_branch, w_out, ln1_g, ln1_b, ln2_g, ln2_b, ffn_wg, ffn_wu, ffn_wd, router_w, moe_wg, moe_wu, moe_wd):
    B, S, D = x.shape
    depth = w_in.shape[0]
    T = B * S
    alpha = (2 * depth) ** 0.25
    rg_w = conv_w.shape[2]
    g_hk, g_hv = GLA_HEADS * GLA_DK, GLA_HEADS * GLA_DV
    rank = gla_wa.shape[1]
    h_hk, h_hv = HG_HEADS * HG_DK, HG_HEADS * HG_DV
    n_br = w_branch.shape[1]
    sizes = (rg_w, g_hk, g_hk, g_hv, rank, g_hv, h_hk, h_hk, h_hv, h_hv, n_br * D)
    offs = [0]
    for sz in sizes:
        offs.append(offs[-1] + sz)
    assert offs[-1] == w_in.shape[2]

    t = _tiles(S, T, ffn_wg.shape[2], moe_wg.shape[3])
    row2 = lambda v: v.reshape(1, -1)
    cast = lambda w: w.astype(MXU_DTYPE)

    x2d = x.reshape(T, D)
    for l in range(depth):
        wl = w_in[l]
        w_rg = cast(wl[:, offs[0]:offs[1]])
        a_cols = jnp.pad(wl[:, offs[4]:offs[5]], ((0, 0), (0, V7X_LANES - rank)))
        w_gla = cast(jnp.concatenate([wl[:, offs[1]:offs[4]], wl[:, offs[5]:offs[6]], a_cols], axis=1))
        w_hg = cast(wl[:, offs[6]:offs[10]])
        w_gate = cast(wl[:, offs[10]:offs[11]].reshape(D, n_br, D).transpose(1, 0, 2))
        wa = cast(jnp.pad(gla_wa[l], ((0, V7X_LANES - rank), (0, 0))))
        x3d = x2d.reshape(B, S, D)

        y_rg, y_gla, y_hg = _mixers(
            x3d,
            (w_rg, conv_w[l], row2(conv_b[l]), cast(_block_diag(rg_wr[l])), row2(rg_br[l]),
             cast(_block_diag(rg_wi[l])), row2(rg_bi[l]), row2(rg_lambda[l])),
            (w_gla, wa, row2(gla_ba[l]), row2(gla_norm_w[l])),
            (w_hg, hg_lb_logits, row2(hg_norm_w[l])),
            ts=t["ts_mix"], layer=l)

        is_moe = l % 2 == 1
        j = l // 2
        merged = _merge(x2d, y_rg.reshape(T, -1), y_gla.reshape(T, -1), y_hg.reshape(T, -1), w_gate,
                        cast(w_branch[l]), cast(w_out[l]), row2(ln1_g[l]), row2(ln1_b[l]),
                        router_w[j] if is_moe else None, alpha=alpha, tm=t["tm_merge"])
        if is_moe:
            x1, route, x1_rows = merged
            x2d = _moe_layer(x1, x1_rows, route, moe_wg[j], moe_wu[j], moe_wd[j],
                             row2(ln2_g[l]), row2(ln2_b[l]), alpha=alpha,
                             bm=t["bm_moe"], tf=t["tf_moe"], tm=t["tm_combine"])
        else:
            x2d = _ffn(merged, ffn_wg[j], ffn_wu[j], ffn_wd[j], row2(ln2_g[l]), row2(ln2_b[l]),
                       alpha=alpha, tm=t["tm_ffn"], tf=t["tf_ffn"])
    return x2d.reshape(B, S, D)
```

```python
import functools

import jax
import jax.numpy as jnp
from jax import lax
from jax.experimental import pallas as pl
from jax.experimental.pallas import tpu as pltpu

F32 = jnp.float32
MXU_DTYPE = jnp.bfloat16

RG_C = 8.0
GLA_HEADS, GLA_DK, GLA_DV = 4, 64, 128
GLA_GATE_NORMALIZER = 16.0
HG_HEADS, HG_DK, HG_DV = 4, 128, 128
HG_F_MIN = 1e-30
TOP_K = 2
LN_EPS = 1e-5
RMS_EPS = 1e-6

CHUNK = 64
SUB = 32
PAD = SUB
SAFE_EXP = 60.0

V7X_LANES = 128
V7X_SUBLANES = 8
V7X_MXU_DIM = 256
V7X_VMEM_BYTES = 64 * 1024 * 1024
VMEM_LIMIT = V7X_VMEM_BYTES * 7 // 8


def _cparams(*sem):
    return pltpu.CompilerParams(dimension_semantics=sem, vmem_limit_bytes=VMEM_LIMIT)


def _mm(a, b):
    return jnp.dot(a.astype(MXU_DTYPE), b.astype(MXU_DTYPE), preferred_element_type=F32)


def _mm_nt(a, b):
    return lax.dot_general(a.astype(MXU_DTYPE), b.astype(MXU_DTYPE), (((1,), (1,)), ((), ())),
                           preferred_element_type=F32)


def _mm_tn(a, b):
    return lax.dot_general(a.astype(MXU_DTYPE), b.astype(MXU_DTYPE), (((0,), (0,)), ((), ())),
                           preferred_element_type=F32)


def _softplus(z):
    return jnp.maximum(z, 0.0) + jnp.log1p(jnp.exp(-jnp.abs(z)))


def _log_sigmoid(z):
    return jnp.minimum(z, 0.0) - jnp.log1p(jnp.exp(-jnp.abs(z)))


def _store_row_tiles(ref, val):
    n, d = val.shape
    per_row = d // V7X_LANES
    for j in range(per_row):
        ref[pl.ds(j, n, stride=per_row), :] = val[:, j * V7X_LANES:(j + 1) * V7X_LANES]


def _load_row_tiles(ref, first_row, n, d):
    per_row = d // V7X_LANES
    return jnp.concatenate(
        [ref[pl.ds(first_row * per_row + j, n, stride=per_row), :] for j in range(per_row)], axis=1)


def _layer_norm(v, g, b):
    mu = jnp.mean(v, axis=-1, keepdims=True)
    d = v - mu
    var = jnp.mean(d * d, axis=-1, keepdims=True)
    return d * lax.rsqrt(var + LN_EPS) * g + b


def _rg_init(tail_ref, h_ref):
    tail_ref[...] = jnp.zeros_like(tail_ref)
    h_ref[...] = jnp.zeros_like(h_ref)


def _rg_body(xm, w_ref, cw_ref, cb_ref, wr_ref, br_ref, wi_ref, bi_ref, lam_ref, y_ref, tail_ref, h_ref):
    ts = xm.shape[0]
    taps, width = cw_ref.shape
    u = _mm(xm, w_ref[...])
    ext = jnp.concatenate([tail_ref[...], u], axis=0)
    c = cb_ref[...]
    for j in range(taps):
        off = V7X_SUBLANES - (taps - 1) + j
        c = c + ext[off:off + ts] * cw_ref[j:j + 1, :]
    tail_ref[...] = u[ts - V7X_SUBLANES:]

    cm = c.astype(MXU_DTYPE)
    nblk = width // V7X_MXU_DIM
    zr = jnp.concatenate([_mm(cm[:, n * V7X_MXU_DIM:(n + 1) * V7X_MXU_DIM], wr_ref[n]) for n in range(nblk)], axis=1)
    zi = jnp.concatenate([_mm(cm[:, n * V7X_MXU_DIM:(n + 1) * V7X_MXU_DIM], wi_ref[n]) for n in range(nblk)], axis=1)
    r = jax.nn.sigmoid(zr + br_ref[...])
    i = jax.nn.sigmoid(zi + bi_ref[...])
    log_a = (-RG_C) * r * _softplus(-lam_ref[...])
    a = jnp.exp(log_a)
    b = jnp.sqrt(jnp.maximum(-jnp.tanh(log_a) * (a * a + 1.0), 0.0)) * (i * c)

    row = lax.broadcasted_iota(jnp.int32, (ts, width), 0)
    shift = 1
    while shift < ts:
        valid = row >= shift
        b = jnp.where(valid, a * pltpu.roll(b, shift, 0) + b, b)
        a = jnp.where(valid, a * pltpu.roll(a, shift, 0), a)
        shift *= 2
    h = b + a * h_ref[...]
    h_ref[...] = h[ts - 1:ts, :]
    y_ref[...] = h.astype(y_ref.dtype)


def _cla_scratch(ts, heads, dk, dv):
    hk, hv = heads * dk, heads * dv
    return [
        pltpu.VMEM((ts, hk), F32),
        pltpu.VMEM((PAD + ts, hk), F32),
        pltpu.VMEM((PAD + ts, hk), F32),
        pltpu.VMEM((PAD + ts, hv), F32),
        pltpu.VMEM((ts, hk), MXU_DTYPE),
        pltpu.VMEM((ts, hv), F32),
        pltpu.VMEM((heads, dv, dk), F32),
    ]


def _cla_init(scratch):
    _, k_s, g_s, v_s, _, _, st_ref = scratch
    st_ref[...] = jnp.zeros_like(st_ref)
    k_s[0:PAD, :] = jnp.zeros((PAD, k_s.shape[1]), F32)
    g_s[0:PAD, :] = jnp.zeros((PAD, g_s.shape[1]), F32)
    v_s[0:PAD, :] = jnp.zeros((PAD, v_s.shape[1]), F32)


def _cla_main(q, k, lg, v, scratch, *, heads, dk, dv):
    q_s, k_s, g_s, v_s, qg_s, o_s, st_ref = scratch
    ts = q.shape[0]
    hk = heads * dk
    nsub = CHUNK // SUB

    row = lax.broadcasted_iota(jnp.int32, (ts, hk), 0)
    in_chunk = row % CHUNK
    g = lg
    shift = 1
    while shift < CHUNK:
        g = g + jnp.where(in_chunk >= shift, pltpu.roll(g, shift, 0), 0.0)
        shift *= 2
    span = g - jnp.where(in_chunk >= SUB, pltpu.roll(g, SUB, 0), 0.0)
    unsafe = jnp.max(-span) > SAFE_EXP

    q_s[...] = q
    k_s[PAD:PAD + ts, :] = k
    g_s[PAD:PAD + ts, :] = g
    v_s[PAD:PAD + ts, :] = v
    qg_s[...] = (q * jnp.exp(g)).astype(MXU_DTYPE)

    ci = lax.broadcasted_iota(jnp.int32, (CHUNK, CHUNK), 0)
    si = lax.broadcasted_iota(jnp.int32, (CHUNK, CHUNK), 1)
    col_limit = jnp.where(unsafe, (ci // SUB) * SUB - 1, ci)
    keep = si <= col_limit

    def chunk_body(n, carry):
        r0 = pl.multiple_of(n * CHUNK, CHUNK)
        gc = g_s[pl.ds(PAD + r0, CHUNK), :]
        kc = k_s[pl.ds(PAD + r0, CHUNK), :]
        vc = v_s[pl.ds(PAD + r0, CHUNK), :].astype(MXU_DTYPE)
        qc = q_s[pl.ds(r0, CHUNK), :]
        qgc = qg_s[pl.ds(r0, CHUNK), :]
        g_last = gc[CHUNK - 1:CHUNK, :]
        k_dec = (kc * jnp.exp(g_last - gc)).astype(MXU_DTYPE)
        dec = jnp.exp(g_last)
        q_sub, k_sub = [], []
        for i in range(nsub):
            rows = slice(SUB * i, SUB * (i + 1))
            ref = gc[SUB * i - 1:SUB * i, :] if i else jnp.zeros((1, hk), F32)
            q_sub.append((qc[rows] * jnp.exp(gc[rows] - ref)).astype(MXU_DTYPE))
            k_sub.append((kc * jnp.exp(jnp.minimum(ref - gc, SAFE_EXP))).astype(MXU_DTYPE))
        for h in range(heads):
            ks = slice(h * dk, (h + 1) * dk)
            vs = slice(h * dv, (h + 1) * dv)
            st = st_ref[h]
            o = _mm_nt(qgc[:, ks], st)
            a = jnp.concatenate([_mm_nt(q_sub[i][:, ks], k_sub[i][:, ks]) for i in range(nsub)], axis=0)
            a = jnp.where(keep, a, 0.0)
            o = o + _mm(a, vc[:, vs])
            o_s[pl.ds(r0, CHUNK), vs] = o
            st_ref[h] = st * dec[:, ks] + _mm_tn(vc[:, vs], k_dec[:, ks])
        return carry

    lax.fori_loop(0, ts // CHUNK, chunk_body, 0, unroll=True)
    return unsafe


def _cla_direct_diagonal(scratch, *, heads, dk, dv):
    q_s, k_s, g_s, v_s, _, o_s, _ = scratch
    ts = q_s.shape[0]
    in_sub = lax.broadcasted_iota(jnp.int32, q_s.shape, 0) % SUB
    g_t = g_s[PAD:PAD + ts, :]
    q_t = q_s[...]
    for d in range(SUB):
        g_sh = g_s[PAD - d:PAD - d + ts, :]
        k_sh = k_s[PAD - d:PAD - d + ts, :]
        w = jnp.where(in_sub >= d, q_t * k_sh * jnp.exp(jnp.minimum(g_t - g_sh, 0.0)), 0.0)
        for h in range(heads):
            coef = jnp.sum(w[:, h * dk:(h + 1) * dk], axis=1, keepdims=True)
            v_sh = v_s[PAD - d:PAD - d + ts, h * dv:(h + 1) * dv]
            o_s[:, h * dv:(h + 1) * dv] += coef * v_sh


def _gla_inputs(xm, w_ref, wa_ref, ba_ref):
    hk, hv = GLA_HEADS * GLA_DK, GLA_HEADS * GLA_DV
    proj = _mm(xm, w_ref[...])
    q = proj[:, :hk] * (GLA_DK ** -0.5)
    k = proj[:, hk:2 * hk]
    v = proj[:, 2 * hk:2 * hk + hv]
    gate = proj[:, 2 * hk + hv:2 * hk + 2 * hv]
    a_low = proj[:, 2 * hk + 2 * hv:]
    lg = _log_sigmoid(_mm(a_low, wa_ref[...]) + ba_ref[...]) * (1.0 / GLA_GATE_NORMALIZER)
    return q, k, lg, v, gate


def _gla_output(o, gate, nw_ref, y_ref):
    outs = []
    for h in range(GLA_HEADS):
        oh = o[:, h * GLA_DV:(h + 1) * GLA_DV]
        ms = jnp.mean(oh * oh, axis=-1, keepdims=True)
        outs.append(oh * lax.rsqrt(ms + RMS_EPS) * nw_ref[...])
    y_ref[...] = (jnp.concatenate(outs, axis=1) * jax.nn.silu(gate)).astype(y_ref.dtype)


def _hg_inputs(xm, w_ref, lbl_ref, layer):
    hk, hv = HG_HEADS * HG_DK, HG_HEADS * HG_DV
    proj = _mm(xm, w_ref[...])
    lbl = lbl_ref[...]
    e = jnp.exp(lbl - jnp.max(lbl, axis=0, keepdims=True))
    lb_w = e / jnp.sum(e, axis=0, keepdims=True)
    lb = jnp.zeros((1, hk), F32)
    for l in range(1, layer + 1):
        lb = lb + lb_w[l:l + 1, :]
    z = proj[:, hk:2 * hk]
    forget = lb + (1.0 - lb) * jax.nn.sigmoid(z)
    lg = jnp.log(jnp.maximum(forget, HG_F_MIN))
    k = (1.0 - lb) * jax.nn.sigmoid(-z)
    q = jax.nn.silu(proj[:, :hk]) * (HG_DK ** -0.5)
    v = proj[:, 2 * hk:2 * hk + hv]
    gate = proj[:, 2 * hk + hv:]
    return q, k, lg, v, gate


def _hg_output(o, gate, nw_ref, y_ref):
    ms = jnp.mean(o * o, axis=-1, keepdims=True)
    y_ref[...] = (o * lax.rsqrt(ms + RMS_EPS) * nw_ref[...] * jax.nn.sigmoid(gate)).astype(y_ref.dtype)


N_RG_PARAMS, N_GLA_PARAMS, N_HG_PARAMS, N_CLA_SCRATCH = 8, 4, 3, 7


def _mixer_kernel(x_ref, *refs, layer):
    take = lambda n: tuple(next(it) for _ in range(n))
    it = iter(refs)
    rg_p, gla_p, hg_p = take(N_RG_PARAMS), take(N_GLA_PARAMS), take(N_HG_PARAMS)
    y_rg, y_gla, y_hg = take(3)
    rg_s, gla_s, hg_s = take(2), take(N_CLA_SCRATCH), take(N_CLA_SCRATCH)
    gla_dims = dict(heads=GLA_HEADS, dk=GLA_DK, dv=GLA_DV)
    hg_dims = dict(heads=HG_HEADS, dk=HG_DK, dv=HG_DV)

    @pl.when(pl.program_id(1) == 0)
    def _():
        _rg_init(*rg_s)
        _cla_init(gla_s)
        _cla_init(hg_s)

    xm = x_ref[...].astype(MXU_DTYPE)
    _rg_body(xm, *rg_p, y_rg, *rg_s)
    gq, gk, glg, gv, g_gate = _gla_inputs(xm, *gla_p[:3])
    gla_unsafe = _cla_main(gq, gk, glg, gv, gla_s, **gla_dims)
    _gla_output(gla_s[5][...], g_gate, gla_p[3], y_gla)
    hq, hk, hlg, hv, h_gate = _hg_inputs(xm, *hg_p[:2], layer)
    hg_unsafe = _cla_main(hq, hk, hlg, hv, hg_s, **hg_dims)
    _hg_output(hg_s[5][...], h_gate, hg_p[2], y_hg)

    @pl.when(gla_unsafe)
    def _():
        _cla_direct_diagonal(gla_s, **gla_dims)
        _gla_output(gla_s[5][...], g_gate, gla_p[3], y_gla)

    @pl.when(hg_unsafe)
    def _():
        _cla_direct_diagonal(hg_s, **hg_dims)
        _hg_output(hg_s[5][...], h_gate, hg_p[2], y_hg)


def _mixers(x, rg_params, gla_params, hg_params, *, ts, layer):
    B, S, D = x.shape
    params = (*rg_params, *gla_params, *hg_params)
    assert (len(rg_params), len(gla_params), len(hg_params)) == (N_RG_PARAMS, N_GLA_PARAMS, N_HG_PARAMS)
    width = rg_params[1].shape[1]
    widths = (width, GLA_HEADS * GLA_DV, HG_HEADS * HG_DV)
    full = lambda shape: pl.BlockSpec(shape, lambda b, s: (0,) * len(shape))
    tile = lambda w: pl.BlockSpec((None, ts, w), lambda b, s: (b, s, 0))
    return pl.pallas_call(
        functools.partial(_mixer_kernel, layer=layer),
        grid=(B, S // ts),
        in_specs=[tile(D)] + [full(p.shape) for p in params],
        out_specs=tuple(tile(w) for w in widths),
        out_shape=tuple(jax.ShapeDtypeStruct((B, S, w), MXU_DTYPE) for w in widths),
        scratch_shapes=[pltpu.VMEM((V7X_SUBLANES, width), F32), pltpu.VMEM((1, width), F32)]
        + _cla_scratch(ts, GLA_HEADS, GLA_DK, GLA_DV) + _cla_scratch(ts, HG_HEADS, HG_DK, HG_DV),
        compiler_params=_cparams("parallel", "arbitrary"),
        name="mixers",
    )(x, *params)


def _merge_kernel(x_ref, yr_ref, yg_ref, yh_ref, wgate_ref, wbr_ref, wo_ref, g_ref, b_ref, *rest,
                  alpha, n_experts):
    if n_experts:
        rw_ref, out_ref, route_ref, out_rows_ref = rest
    else:
        (out_ref,) = rest
    x = x_ref[...]
    xm = x.astype(MXU_DTYPE)
    mixed = None
    for n, y_ref in enumerate((yr_ref, yg_ref, yh_ref)):
        term = jax.nn.sigmoid(_mm(xm, wgate_ref[n])) * _mm(y_ref[...], wbr_ref[n])
        mixed = term if mixed is None else mixed + term
    out = _layer_norm(alpha * x + _mm(mixed, wo_ref[...]), g_ref[...], b_ref[...])
    out_ref[...] = out
    if n_experts:
        _store_row_tiles(out_rows_ref, out)
        x_hi = out.astype(MXU_DTYPE)
        x_lo = out - x_hi.astype(F32)
        rw = rw_ref[...]
        p_hi = _mm(x_hi, rw)
        logits = p_hi[:, :V7X_LANES] + p_hi[:, V7X_LANES:] + _mm(x_lo, rw[:, :V7X_LANES])
        lane = lax.broadcasted_iota(jnp.int32, logits.shape, 1)
        neg = jnp.float32(-jnp.inf)
        l1 = jnp.where(lane < n_experts, logits, neg)
        m1 = jnp.max(l1, axis=1, keepdims=True)
        i1 = jnp.min(jnp.where(l1 == m1, lane, V7X_LANES), axis=1, keepdims=True)
        l2 = jnp.where(lane == i1, neg, l1)
        m2 = jnp.max(l2, axis=1, keepdims=True)
        i2 = jnp.min(jnp.where(l2 == m2, lane, V7X_LANES), axis=1, keepdims=True)
        e2 = jnp.exp(m2 - m1)
        w1 = 1.0 / (1.0 + e2)
        w2 = e2 / (1.0 + e2)
        route = jnp.where(lane == 0, i1.astype(F32),
                          jnp.where(lane == 1, i2.astype(F32),
                                    jnp.where(lane == 2, w1, jnp.where(lane == 3, w2, 0.0))))
        route_ref[...] = route


def _merge(x2d, y_rg, y_gla, y_hg, w_gate, w_branch, w_out, ln_g, ln_b, router_w, *, alpha, tm):
    T, D = x2d.shape
    bw = y_rg.shape[1]
    n_experts = 0 if router_w is None else router_w.shape[1]
    full = lambda shape: pl.BlockSpec(shape, lambda i: (0,) * len(shape))
    row = lambda width: pl.BlockSpec((tm, width), lambda i: (i, 0))
    in_specs = [row(D), row(bw), row(bw), row(bw), full(w_gate.shape), full(w_branch.shape),
                full(w_out.shape), full((1, D)), full((1, D))]
    args = [x2d, y_rg, y_gla, y_hg, w_gate, w_branch, w_out, ln_g, ln_b]
    out_specs = row(D)
    out_shape = jax.ShapeDtypeStruct((T, D), F32)
    if n_experts:
        rw = jnp.pad(router_w, ((0, 0), (0, V7X_LANES - n_experts)))
        rw_hi = rw.astype(MXU_DTYPE)
        rw = jnp.concatenate([rw_hi, (rw - rw_hi.astype(F32)).astype(MXU_DTYPE)], axis=1)
        in_specs.append(full(rw.shape))
        args.append(rw)
        per_row = D // V7X_LANES
        out_specs = (out_specs, row(V7X_LANES), pl.BlockSpec((tm * per_row, V7X_LANES), lambda i: (i, 0)))
        out_shape = (out_shape, jax.ShapeDtypeStruct((T, V7X_LANES), F32),
                     jax.ShapeDtypeStruct((T * per_row, V7X_LANES), F32))
    return pl.pallas_call(
        functools.partial(_merge_kernel, alpha=alpha, n_experts=n_experts),
        grid=(T // tm,),
        in_specs=in_specs,
        out_specs=out_specs,
        out_shape=out_shape,
        compiler_params=_cparams("parallel"),
        name="merge_router" if n_experts else "merge",
    )(*args)


def _ffn_kernel(x_ref, wg_ref, wu_ref, wd_ref, g_ref, b_ref, o_ref, xm_s, acc_s, *, alpha):
    f = pl.program_id(1)

    @pl.when(f == 0)
    def _():
        xm_s[...] = x_ref[...].astype(MXU_DTYPE)
        acc_s[...] = jnp.zeros_like(acc_s)

    xm = xm_s[...]
    hidden = jax.nn.silu(_mm(xm, wg_ref[...])) * _mm(xm, wu_ref[...])
    acc_s[...] += _mm(hidden, wd_ref[...])

    @pl.when(f == pl.num_programs(1) - 1)
    def _():
        o_ref[...] = _layer_norm(alpha * x_ref[...] + acc_s[...], g_ref[...], b_ref[...])


def _ffn(x2d, wg, wu, wd, ln_g, ln_b, *, alpha, tm, tf):
    T, D = x2d.shape
    F = wg.shape[1]
    return pl.pallas_call(
        functools.partial(_ffn_kernel, alpha=alpha),
        grid=(T // tm, F // tf),
        in_specs=[
            pl.BlockSpec((tm, D), lambda i, f: (i, 0)),
            pl.BlockSpec((D, tf), lambda i, f: (0, f)),
            pl.BlockSpec((D, tf), lambda i, f: (0, f)),
            pl.BlockSpec((tf, D), lambda i, f: (f, 0)),
            pl.BlockSpec((1, D), lambda i, f: (0, 0)),
            pl.BlockSpec((1, D), lambda i, f: (0, 0)),
        ],
        out_specs=pl.BlockSpec((tm, D), lambda i, f: (i, 0)),
        out_shape=jax.ShapeDtypeStruct((T, D), F32),
        scratch_shapes=[pltpu.VMEM((tm, D), MXU_DTYPE), pltpu.VMEM((tm, D), F32)],
        compiler_params=_cparams("parallel", "arbitrary"),
        name="ffn_swiglu",
    )(x2d, wg, wu, wd, ln_g, ln_b)


def _row_copy(src_hbm, src_row, dst, dst_row, sem):
    per_row = src_hbm.shape[1]
    return pltpu.make_async_copy(src_hbm.at[src_row], dst.at[pl.ds(dst_row * per_row, per_row), :], sem)


def _moe_kernel(be_ref, nb_ref, tok_ref, tok_next_ref, x_hbm, wg_ref, wu_ref, wd_ref, o_ref,
                xg_s, sem, xm_s, acc_s, *, bm, per_step):
    i = pl.program_id(0)
    f = pl.program_id(1)
    nf = pl.num_programs(1)
    n_used = nb_ref[0]
    slot = lax.rem(i, 2)
    d = xm_s.shape[1]

    def start_rows(t_ref, sl, step):
        for j in range(per_step):
            r = step * per_step + j
            _row_copy(x_hbm, t_ref[0, 0, jnp.minimum(r, bm - 1)], xg_s.at[sl], r, sem.at[sl]).start()

    @pl.when(jnp.logical_and(f == 0, i == 0))
    def _():
        def body(step, c):
            start_rows(tok_ref, 0, step)
            return c
        lax.fori_loop(0, nf, body, 0)

    @pl.when(jnp.logical_and(f == 0, i <= n_used))
    def _():
        for r in range(xg_s.shape[1] // x_hbm.shape[1]):
            _row_copy(x_hbm, 0, xg_s.at[slot], r, sem.at[slot]).wait()
        xm_s[...] = _load_row_tiles(xg_s.at[slot], 0, bm, d).astype(MXU_DTYPE)
        acc_s[...] = jnp.zeros_like(acc_s)

    @pl.when(i < n_used)
    def _():
        start_rows(tok_next_ref, 1 - slot, f)
        xm = xm_s[...]
        hidden = jax.nn.silu(_mm(xm, wg_ref[...])) * _mm(xm, wu_ref[...])
        acc_s[...] += _mm(hidden, wd_ref[...])

    last = f == nf - 1

    @pl.when(jnp.logical_and(last, i < n_used))
    def _():
        _store_row_tiles(o_ref, acc_s[...])

    @pl.when(jnp.logical_and(last, i >= n_used))
    def _():
        o_ref[...] = jnp.zeros_like(o_ref)


def _moe_experts(x_rows, block_expert, n_used, buf_tok, wg, wu, wd, *, bm, tf):
    _, per_row, _ = x_rows.shape
    E, D, F = wg.shape
    nb = buf_tok.shape[0]
    nf = F // tf
    per_step = pl.cdiv(bm, nf)

    def w_idx(i, f, be, nu):
        return jnp.where(i < nu[0], f, nf - 1)

    grid_spec = pltpu.PrefetchScalarGridSpec(
        num_scalar_prefetch=2,
        grid=(nb, nf),
        in_specs=[
            pl.BlockSpec((1, 1, bm), lambda i, f, be, nu: (i, 0, 0), memory_space=pltpu.SMEM),
            pl.BlockSpec((1, 1, bm), lambda i, f, be, nu: (jnp.minimum(i + 1, nb - 1), 0, 0),
                         memory_space=pltpu.SMEM),
            pl.BlockSpec(memory_space=pl.ANY),
            pl.BlockSpec((None, D, tf), lambda i, f, be, nu: (be[i], 0, w_idx(i, f, be, nu))),
            pl.BlockSpec((None, D, tf), lambda i, f, be, nu: (be[i], 0, w_idx(i, f, be, nu))),
            pl.BlockSpec((None, tf, D), lambda i, f, be, nu: (be[i], w_idx(i, f, be, nu), 0)),
        ],
        out_specs=pl.BlockSpec((bm * per_row, V7X_LANES), lambda i, f, be, nu: (i, 0)),
        scratch_shapes=[
            pltpu.VMEM((2, nf * per_step * per_row, V7X_LANES), F32),
            pltpu.SemaphoreType.DMA((2,)),
            pltpu.VMEM((bm, D), MXU_DTYPE),
            pltpu.VMEM((bm, D), F32),
        ],
    )
    return pl.pallas_call(
        functools.partial(_moe_kernel, bm=bm, per_step=per_step),
        grid_spec=grid_spec,
        out_shape=jax.ShapeDtypeStruct((nb * bm * per_row, V7X_LANES), F32),
        compiler_params=_cparams("arbitrary", "arbitrary"),
        name="moe_experts",
    )(block_expert, n_used, buf_tok, buf_tok, x_rows, wg, wu, wd)


def _combine_kernel(pos_ref, pos_next_ref, ys_hbm, x_ref, route_ref, g_ref, b_ref, o_ref, yg_s, sem,
                    *, tm, alpha):
    i = pl.program_id(0)
    n = pl.num_programs(0)
    slot = lax.rem(i, 2)
    rows = TOP_K * tm

    def start_gather(p_ref, sl):
        def body(r, c):
            _row_copy(ys_hbm, p_ref[0, 0, r], yg_s.at[sl], r, sem.at[sl]).start()
            return c
        lax.fori_loop(0, rows, body, 0, unroll=8)

    @pl.when(i == 0)
    def _():
        start_gather(pos_ref, 0)

    @pl.when(i + 1 < n)
    def _():
        start_gather(pos_next_ref, 1 - slot)

    for r in range(rows):
        _row_copy(ys_hbm, 0, yg_s.at[slot], r, sem.at[slot]).wait()
    route = route_ref[...]
    x = x_ref[...]
    y = alpha * x
    for k in range(TOP_K):
        y = y + route[:, TOP_K + k:TOP_K + k + 1] * _load_row_tiles(yg_s.at[slot], k * tm, tm, x.shape[1])
    o_ref[...] = _layer_norm(y, g_ref[...], b_ref[...])


def _moe_combine(ys_rows, pos, x2d, route, ln_g, ln_b, *, alpha, tm):
    T, D = x2d.shape
    n = T // tm
    per_row = ys_rows.shape[1]
    return pl.pallas_call(
        functools.partial(_combine_kernel, tm=tm, alpha=alpha),
        grid=(n,),
        in_specs=[
            pl.BlockSpec((1, 1, TOP_K * tm), lambda i: (i, 0, 0), memory_space=pltpu.SMEM),
            pl.BlockSpec((1, 1, TOP_K * tm), lambda i: (jnp.minimum(i + 1, n - 1), 0, 0),
                         memory_space=pltpu.SMEM),
            pl.BlockSpec(memory_space=pl.ANY),
            pl.BlockSpec((tm, D), lambda i: (i, 0)),
            pl.BlockSpec((tm, V7X_LANES), lambda i: (i, 0)),
            pl.BlockSpec((1, D), lambda i: (0, 0)),
            pl.BlockSpec((1, D), lambda i: (0, 0)),
        ],
        out_specs=pl.BlockSpec((tm, D), lambda i: (i, 0)),
        out_shape=jax.ShapeDtypeStruct((T, D), F32),
        scratch_shapes=[pltpu.VMEM((2, TOP_K * tm * per_row, V7X_LANES), F32), pltpu.SemaphoreType.DMA((2,))],
        compiler_params=_cparams("arbitrary"),
        name="moe_combine",
    )(pos, pos, ys_rows, x2d, route, ln_g, ln_b)


def _moe_layer(x2d, x_rows, route, wg, wu, wd, ln_g, ln_b, *, alpha, bm, tf, tm):
    T, D = x2d.shape
    E = wg.shape[0]
    tk = T * TOP_K
    per_row = D // V7X_LANES
    nb = tk // bm + E + 1
    expert = route[:, :TOP_K].astype(jnp.int32).reshape(tk)
    onehot = (expert[:, None] == jnp.arange(E, dtype=jnp.int32)[None, :]).astype(jnp.int32)
    csum = jnp.cumsum(onehot, axis=0)
    counts = csum[-1]
    rank = jnp.sum((csum - onehot) * onehot, axis=1)
    padded = (counts + bm - 1) // bm * bm
    pad_end = jnp.cumsum(padded)
    pad_start = pad_end - padded
    dest = jnp.sum(pad_start[None, :] * onehot, axis=1) + rank
    token = jnp.arange(tk, dtype=jnp.int32) // TOP_K
    buf_tok = jnp.zeros((nb * bm,), jnp.int32).at[dest].set(token)
    block_start = jnp.arange(nb, dtype=jnp.int32) * bm
    block_expert = jnp.minimum(jnp.searchsorted(pad_end, block_start, side="right"), E - 1).astype(jnp.int32)
    n_used = (pad_end[-1] // bm).astype(jnp.int32).reshape(1)
    ys = _moe_experts(x_rows.reshape(T, per_row, V7X_LANES), block_expert, n_used, buf_tok.reshape(nb, 1, bm),
                      wg, wu, wd, bm=bm, tf=tf)
    pos = dest.reshape(T // tm, tm, TOP_K).transpose(0, 2, 1).reshape(T // tm, 1, TOP_K * tm)
    return _moe_combine(ys.reshape(nb * bm, per_row, V7X_LANES), pos, x2d, route, ln_g, ln_b, alpha=alpha, tm=tm)


def _block_diag(w):
    g, n, _ = w.shape
    per = V7X_MXU_DIM // n
    w = w.reshape(g // per, per, n, n)
    eye = jnp.eye(per, dtype=w.dtype)
    return jnp.einsum("apij,pq->apiqj", w, eye).reshape(g // per, V7X_MXU_DIM, V7X_MXU_DIM)


def _pick(n, pref):
    t = min(n, pref)
    while n % t:
        t //= 2
    return t


def _tiles(S, T, f_dense, f_expert):
    return dict(
        ts_mix=_pick(S, 256),
        tm_merge=_pick(T, 512),
        tm_ffn=_pick(T, 1024), tf_ffn=_pick(f_dense, V7X_MXU_DIM),
        bm_moe=_pick(T * TOP_K, 1024), tf_moe=_pick(f_expert, 2 * V7X_MXU_DIM),
        tm_combine=_pick(T, 256),
    )


def kernel(x, w_in, conv_w, conv_b, rg_wr, rg_br, rg_wi, rg_bi, rg_lambda, gla_wa, gla_ba, gla_norm_w, hg_lb_logits, hg_norm_w, w_branch, w_out, ln1_g, ln1_b, ln2_g, ln2_b, ffn_wg, ffn_wu, ffn_wd, router_w, moe_wg, moe_wu, moe_wd):
    B, S, D = x.shape
    depth = w_in.shape[0]
    T = B * S
    alpha = (2 * depth) ** 0.25
    rg_w = conv_w.shape[2]
    g_hk, g_hv = GLA_HEADS * GLA_DK, GLA_HEADS * GLA_DV
    rank = gla_wa.shape[1]
    h_hk, h_hv = HG_HEADS * HG_DK, HG_HEADS * HG_DV
    n_br = w_branch.shape[1]
    sizes = (rg_w, g_hk, g_hk, g_hv, rank, g_hv, h_hk, h_hk, h_hv, h_hv, n_br * D)
    offs = [0]
    for sz in sizes:
        offs.append(offs[-1] + sz)
    assert offs[-1] == w_in.shape[2]

    t = _tiles(S, T, ffn_wg.shape[2], moe_wg.shape[3])
    row2 = lambda v: v.reshape(1, -1)
    cast = lambda w: w.astype(MXU_DTYPE)

    x2d = x.reshape(T, D)
    for l in range(depth):
        wl = w_in[l]
        w_rg = cast(wl[:, offs[0]:offs[1]])
        a_cols = jnp.pad(wl[:, offs[4]:offs[5]], ((0, 0), (0, V7X_LANES - rank)))
        w_gla = cast(jnp.concatenate([wl[:, offs[1]:offs[4]], wl[:, offs[5]:offs[6]], a_cols], axis=1))
        w_hg = cast(wl[:, offs[6]:offs[10]])
        w_gate = cast(wl[:, offs[10]:offs[11]].reshape(D, n_br, D).transpose(1, 0, 2))
        wa = cast(jnp.pad(gla_wa[l], ((0, V7X_LANES - rank), (0, 0))))
        x3d = x2d.reshape(B, S, D)

        y_rg, y_gla, y_hg = _mixers(
            x3d,
            (w_rg, conv_w[l], row2(conv_b[l]), cast(_block_diag(rg_wr[l])), row2(rg_br[l]),
             cast(_block_diag(rg_wi[l])), row2(rg_bi[l]), row2(rg_lambda[l])),
            (w_gla, wa, row2(gla_ba[l]), row2(gla_norm_w[l])),
            (w_hg, hg_lb_logits, row2(hg_norm_w[l])),
            ts=t["ts_mix"], layer=l)

        is_moe = l % 2 == 1
        j = l // 2
        merged = _merge(x2d, y_rg.reshape(T, -1), y_gla.reshape(T, -1), y_hg.reshape(T, -1), w_gate,
                        cast(w_branch[l]), cast(w_out[l]), row2(ln1_g[l]), row2(ln1_b[l]),
                        router_w[j] if is_moe else None, alpha=alpha, tm=t["tm_merge"])
        if is_moe:
            x1, route, x1_rows = merged
            x2d = _moe_layer(x1, x1_rows, route, moe_wg[j], moe_wu[j], moe_wd[j],
                             row2(ln2_g[l]), row2(ln2_b[l]), alpha=alpha,
                             bm=t["bm_moe"], tf=t["tf_moe"], tm=t["tm_combine"])
        else:
            x2d = _ffn(merged, ffn_wg[j], ffn_wu[j], ffn_wd[j], row2(ln2_g[l]), row2(ln2_b[l]),
                       alpha=alpha, tm=t["tm_ffn"], tf=t["tf_ffn"])
    return x2d.reshape(B, S, D)
```
